```python
import math
import jax, jax.numpy as jnp
from jax import lax
import numpy as np

D_MODEL = 2048
BATCH = 4
SEQ = 8192
DEPTH = 2
DEC_BATCH = 4
DEC_SEQ = 2048
PAST_LEN = 128

SSD_D_INNER = D_MODEL
SSD_HEAD_DIM = 64
SSD_HEADS = SSD_D_INNER // SSD_HEAD_DIM
SSD_GROUPS = 4
SSD_HEADS_PER_GROUP = SSD_HEADS // SSD_GROUPS
SSD_STATE = 128
SSD_CHUNK = 128
SSD_CONV_DIM = SSD_D_INNER + 2 * SSD_GROUPS * SSD_STATE
GDN_HEADS = 16
GDN_HEAD_K = 128
GDN_HEAD_V = 128
GDN_KEY_DIM = GDN_HEADS * GDN_HEAD_K
GDN_VAL_DIM = GDN_HEADS * GDN_HEAD_V
GDN_CHUNK = 64
GDN_CONV_DIM = 2 * GDN_KEY_DIM + GDN_VAL_DIM
CONV_W = 5
IN_SIZES = (SSD_D_INNER, SSD_CONV_DIM, 2 * SSD_HEADS, GDN_CONV_DIM, GDN_VAL_DIM, 2 * GDN_HEADS, 2 * GDN_HEADS)
IN_COLS = SSD_D_INNER + SSD_CONV_DIM + 2 * SSD_HEADS + GDN_CONV_DIM + GDN_VAL_DIM + 4 * GDN_HEADS
MIX_WIDTH = SSD_D_INNER + GDN_VAL_DIM
POOL_WINDOWS = (2, 4, 8, 16)
POOL_GROUPS = 4
POOL_GROUP_DIM = D_MODEL // POOL_GROUPS
D_FF = 4 * D_MODEL
EPS = 1e-6
N_EVEN = (DEPTH + 1) // 2
N_ODD = DEPTH // 2

kernel_name = 'bidir_ssd_deltanet_pool_hybrid'

F32 = jnp.float32


def rms_norm(x, w):
    xf = x.astype(F32)
    y = xf * lax.rsqrt(jnp.mean(xf * xf, axis=-1, keepdims=True) + EPS)
    return (y * w.astype(F32)).astype(x.dtype)


def rms_norm_f32(x, w):
    y = x * lax.rsqrt(jnp.mean(x * x, axis=-1, keepdims=True) + EPS)
    return y * w.astype(F32)


def group_rms_norm(x, w, groups):
    shp = x.shape
    xg = x.reshape(shp[:-1] + (groups, shp[-1] // groups))
    xg = xg * lax.rsqrt(jnp.mean(xg * xg, axis=-1, keepdims=True) + EPS)
    return xg.reshape(shp) * w.astype(F32)


def l2_normalize(x):
    return x * lax.rsqrt(jnp.sum(x * x, axis=-1, keepdims=True) + EPS)


def rev(t):
    return jnp.flip(t, axis=1)


def depthwise_conv_centred(x, w):
    c = x.shape[-1]
    return lax.conv_general_dilated(
        x.astype(F32), w.astype(F32)[:, None, :], window_strides=(1,),
        padding=((CONV_W // 2, CONV_W // 2),), dimension_numbers=('NWC', 'WIO', 'NWC'),
        feature_group_count=c)


def ssd_chunked(x, dt, a, bm, cm):
    b, L = x.shape[:2]
    c, q = L // SSD_CHUNK, SSD_CHUNK
    g, r, p, n = SSD_GROUPS, SSD_HEADS_PER_GROUP, SSD_HEAD_DIM, SSD_STATE
    xc = (x * dt[..., None]).reshape(b, c, q, g, r, p)
    ac = (dt * a).reshape(b, c, q, g, r)
    bc = bm.reshape(b, c, q, g, n)
    cc = cm.reshape(b, c, q, g, n)
    cs = jnp.cumsum(ac, axis=2)
    seg = cs[:, :, :, None] - cs[:, :, None, :]
    tril = jnp.tril(jnp.ones((q, q), dtype=bool))[:, :, None, None]
    lmat = jnp.exp(jnp.where(tril, seg, -jnp.inf))
    cb = jnp.einsum('bclgn,bcsgn->bclsg', cc, bc)
    y_diag = jnp.einsum('bclsgr,bcsgrp->bclgrp', cb[..., None] * lmat, xc)
    decay_states = jnp.exp(cs[:, :, -1:] - cs)
    states = jnp.einsum('bclgn,bclgrp->bcgrpn', bc, xc * decay_states[..., None])
    chunk_decay = jnp.exp(cs[:, :, -1])

    def step(carry, inp):
        st, dec = inp
        return carry * dec[..., None, None] + st, carry

    init = jnp.zeros((b, g, r, p, n), F32)
    _, prev = lax.scan(step, init, (jnp.moveaxis(states, 1, 0), jnp.moveaxis(chunk_decay, 1, 0)))
    prev = jnp.moveaxis(prev, 0, 1)
    y_off = jnp.einsum('bclgn,bcgrpn->bclgrp', cc, prev) * jnp.exp(cs)[..., None]
    return (y_diag + y_off).reshape(b, L, SSD_HEADS, p)


def ssd_mixer(z, xbc, dt_raw, conv_w, conv_b, a_log, dt_bias, d_skip, norm_w):
    b, L, _ = xbc.shape
    gn = SSD_GROUPS * SSD_STATE
    xbc = jax.nn.silu(depthwise_conv_centred(xbc, conv_w) + conv_b.astype(F32))
    xs = xbc[..., :SSD_D_INNER].reshape(b, L, SSD_HEADS, SSD_HEAD_DIM)
    bm = xbc[..., SSD_D_INNER:SSD_D_INNER + gn].reshape(b, L, SSD_GROUPS, SSD_STATE)
    cm = xbc[..., SSD_D_INNER + gn:].reshape(b, L, SSD_GROUPS, SSD_STATE)
    dt = jax.nn.softplus(dt_raw.astype(F32).reshape(b, L, 2, SSD_HEADS) + dt_bias.astype(F32))
    a = -jnp.exp(a_log.astype(F32))
    y_fwd = ssd_chunked(xs, dt[:, :, 0], a[0], bm, cm)
    y_bwd = rev(ssd_chunked(rev(xs), rev(dt[:, :, 1]), a[1], rev(bm), rev(cm)))
    y = y_fwd + y_bwd + d_skip.astype(F32)[:, None] * xs
    y = y.reshape(b, L, SSD_D_INNER) * jax.nn.silu(z.astype(F32))
    return group_rms_norm(y, norm_w, SSD_GROUPS)


def gated_delta_chunked(q, k, v, g, beta):
    b, L, h, kd = q.shape
    vd = v.shape[-1]
    c, qn = L // GDN_CHUNK, GDN_CHUNK

    def to_chunks(t):
        t = jnp.moveaxis(t.astype(F32), 1, 2)
        return t.reshape((b, h, c, qn) + t.shape[3:])

    q = to_chunks(q) * (kd ** -0.5)
    k = to_chunks(k)
    v = to_chunks(v)
    g = jnp.cumsum(to_chunks(g), axis=-1)
    beta = to_chunks(beta)
    tril = jnp.tril(jnp.ones((qn, qn), dtype=bool))
    strict = jnp.tril(jnp.ones((qn, qn), dtype=bool), -1)
    decay = jnp.exp(jnp.where(tril, g[..., :, None] - g[..., None, :], -jnp.inf))
    k_beta = k * beta[..., None]
    v_beta = v * beta[..., None]
    m = jnp.where(strict, jnp.einsum('bhcik,bhcjk->bhcij', k_beta, k) * decay, 0.0)
    eye = jnp.eye(qn, dtype=F32)
    t_inv = lax.linalg.triangular_solve(eye + m, jnp.broadcast_to(eye, m.shape),
                                        left_side=True, lower=True, unit_diagonal=True)
    u = jnp.einsum('bhcij,bhcjv->bhciv', t_inv, v_beta)
    w = jnp.einsum('bhcij,bhcjk->bhcik', t_inv, k_beta * jnp.exp(g)[..., None])
    attn = jnp.where(tril, jnp.einsum('bhcik,bhcjk->bhcij', q, k) * decay, 0.0)
    g_last = g[..., -1]
    k_dec = k * jnp.exp(g_last[..., None] - g)[..., None]
    q_dec = q * jnp.exp(g)[..., None]

    def step(s, inp):
        q_i, w_i, u_i, a_i, kd_i, gl_i = inp
        v_new = u_i - jnp.einsum('bhik,bhkv->bhiv', w_i, s)
        o = jnp.einsum('bhik,bhkv->bhiv', q_i, s) + jnp.einsum('bhij,bhjv->bhiv', a_i, v_new)
        s = s * jnp.exp(gl_i)[..., None, None] + jnp.einsum('bhik,bhiv->bhkv', kd_i, v_new)
        return s, o

    xs = tuple(jnp.moveaxis(t, 2, 0) for t in (q_dec, w, u, attn, k_dec, g_last))
    s0 = jnp.zeros((b, h, kd, vd), F32)
    _, o = lax.scan(step, s0, xs)
    o = jnp.moveaxis(o, 0, 2).reshape(b, h, L, vd)
    return jnp.moveaxis(o, 1, 2)


def gdn_mixer(qkv, z, beta_raw, a_raw, conv_w, a_log, dt_bias, norm_w):
    b, L, _ = qkv.shape
    qkv = jax.nn.silu(depthwise_conv_centred(qkv, conv_w))
    q = l2_normalize(qkv[..., :GDN_KEY_DIM].reshape(b, L, GDN_HEADS, GDN_HEAD_K))
    k = l2_normalize(qkv[..., GDN_KEY_DIM:2 * GDN_KEY_DIM].reshape(b, L, GDN_HEADS, GDN_HEAD_K))
    v = qkv[..., 2 * GDN_KEY_DIM:].reshape(b, L, GDN_HEADS, GDN_HEAD_V)
    beta = jax.nn.sigmoid(beta_raw.astype(F32).reshape(b, L, 2, GDN_HEADS))
    g = -jnp.exp(a_log.astype(F32)) * jax.nn.softplus(
        a_raw.astype(F32).reshape(b, L, 2, GDN_HEADS) + dt_bias.astype(F32))
    o_fwd = gated_delta_chunked(q, k, v, g[:, :, 0], beta[:, :, 0])
    o_bwd = rev(gated_delta_chunked(rev(q), rev(k), rev(v), rev(g[:, :, 1]), rev(beta[:, :, 1])))
    o = rms_norm_f32(o_fwd + o_bwd, norm_w)
    o = o * jax.nn.silu(z.astype(F32).reshape(b, L, GDN_HEADS, GDN_HEAD_V))
    return o.reshape(b, L, GDN_VAL_DIM)


def ssd_gdn_layer(h, w_in, ssd_conv_w, ssd_conv_b, ssd_a_log, ssd_dt_bias, ssd_d, ssd_norm,
                  gdn_conv_w, gdn_a_log, gdn_dt_bias, gdn_norm, w_out):
    proj = h @ w_in
    splits = np.cumsum(IN_SIZES)[:-1].tolist()
    z_a, xbc_a, dt_a, qkv_b, z_b, beta_b, a_b = jnp.split(proj, splits, axis=-1)
    y_a = ssd_mixer(z_a, xbc_a, dt_a, ssd_conv_w, ssd_conv_b, ssd_a_log, ssd_dt_bias, ssd_d, ssd_norm)
    y_b = gdn_mixer(qkv_b, z_b, beta_b, a_b, gdn_conv_w, gdn_a_log, gdn_dt_bias, gdn_norm)
    y = jnp.concatenate([y_a, y_b], axis=-1).astype(h.dtype)
    return y @ w_out


def pool_mixer(x, pool_w, pool_scale):
    b, L, d = x.shape
    xf = x.astype(F32)
    cs = jnp.concatenate([jnp.zeros((b, 1, d), F32), jnp.cumsum(xf, axis=1)], axis=1)
    t = jnp.arange(L)
    outs = []
    for gi, win in enumerate(POOL_WINDOWS):
        sl = slice(gi * POOL_GROUP_DIM, (gi + 1) * POOL_GROUP_DIM)
        lo = jnp.clip(t - win // 2, 0, L)
        hi = jnp.clip(t + win // 2, 0, L)
        csg = cs[..., sl]
        s = jnp.take(csg, hi, axis=1) - jnp.take(csg, lo, axis=1)
        cnt = (hi - lo).astype(F32)[None, :, None]
        outs.append(s / cnt - xf[..., sl])
    p = jnp.stack(outs, axis=2)
    y = jnp.einsum('blgi,gio->blgo', p, pool_w.astype(F32)).reshape(b, L, d)
    return (y * pool_scale.astype(F32)).astype(x.dtype)


def sq_relu_mlp(x, w1, w2):
    hdn = jnp.square(jax.nn.relu(x @ w1))
    return hdn @ w2


def trunk(x, mix_norm, w_in, ssd_conv_w, ssd_conv_b, ssd_a_log, ssd_dt_bias, ssd_d, ssd_norm,
          gdn_conv_w, gdn_a_log, gdn_dt_bias, gdn_norm, w_out, pool_w, pool_scale,
          ffn_norm, w_ff1, w_ff2, final_norm):
    for i in range(DEPTH):
        j = i // 2
        h = rms_norm(x, mix_norm[i])
        if i % 2 == 0:
            mix = ssd_gdn_layer(h, w_in[j], ssd_conv_w[j], ssd_conv_b[j], ssd_a_log[j], ssd_dt_bias[j],
                                ssd_d[j], ssd_norm[j], gdn_conv_w[j], gdn_a_log[j], gdn_dt_bias[j],
                                gdn_norm[j], w_out[j])
        else:
            mix = pool_mixer(h, pool_w[j], pool_scale[j])
        x = x + mix.astype(x.dtype)
        x = x + sq_relu_mlp(rms_norm(x, ffn_norm[i]), w_ff1[i], w_ff2[i]).astype(x.dtype)
    return rms_norm(x, final_norm)


def setup_inputs(seed: int = 0) -> dict:
    key = jax.random.key(seed)
    ks = jax.random.split(key, 24)

    def nrm(k, shape, fan_in):
        return jax.random.normal(k, shape, F32) * (fan_in ** -0.5)

    def gain(k, shape):
        return 1.0 + 0.02 * jax.random.normal(k, shape, F32)

    def dt_bias_init(k, shape):
        dt = jnp.exp(jax.random.uniform(k, shape, F32, math.log(1e-3), math.log(1e-1)))
        return dt + jnp.log(-jnp.expm1(-dt))

    return {
        'x_prompt': jax.random.normal(ks[0], (BATCH, SEQ, D_MODEL), F32),
        'x_sample': jax.random.normal(ks[1], (DEC_BATCH, DEC_SEQ, D_MODEL), F32),
        'mix_norm': gain(ks[2], (DEPTH, D_MODEL)),
        'w_in': nrm(ks[3], (N_EVEN, D_MODEL, IN_COLS), D_MODEL),
        'ssd_conv_w': nrm(ks[4], (N_EVEN, CONV_W, SSD_CONV_DIM), CONV_W),
        'ssd_conv_b': 0.02 * jax.random.normal(ks[5], (N_EVEN, SSD_CONV_DIM), F32),
        'ssd_a_log': jnp.log(jax.random.uniform(ks[6], (N_EVEN, 2, SSD_HEADS), F32, 1.0, 16.0)),
        'ssd_dt_bias': dt_bias_init(ks[7], (N_EVEN, 2, SSD_HEADS)),
        'ssd_d': gain(ks[8], (N_EVEN, SSD_HEADS)),
        'ssd_norm': gain(ks[9], (N_EVEN, SSD_D_INNER)),
        'gdn_conv_w': nrm(ks[10], (N_EVEN, CONV_W, GDN_CONV_DIM), CONV_W),
        'gdn_a_log': jnp.log(jax.random.uniform(ks[11], (N_EVEN, 2, GDN_HEADS), F32, 1.0, 16.0)),
        'gdn_dt_bias': dt_bias_init(ks[12], (N_EVEN, 2, GDN_HEADS)),
        'gdn_norm': gain(ks[13], (N_EVEN, GDN_HEAD_V)),
        'w_out': nrm(ks[14], (N_EVEN, MIX_WIDTH, D_MODEL), MIX_WIDTH),
        'pool_w': nrm(ks[15], (N_ODD, POOL_GROUPS, POOL_GROUP_DIM, POOL_GROUP_DIM), POOL_GROUP_DIM),
        'pool_scale': 1.0 + 0.1 * jax.random.normal(ks[16], (N_ODD, D_MODEL), F32),
        'ffn_norm': gain(ks[17], (DEPTH, D_MODEL)),
        'w_ff1': nrm(ks[18], (DEPTH, D_MODEL, D_FF), D_MODEL),
        'w_ff2': nrm(ks[19], (DEPTH, D_FF, D_MODEL), D_FF),
        'final_norm': gain(ks[20], (D_MODEL,)),
    }


def reference(x_prompt, x_sample, mix_norm, w_in, ssd_conv_w, ssd_conv_b, ssd_a_log, ssd_dt_bias, ssd_d,
              ssd_norm, gdn_conv_w, gdn_a_log, gdn_dt_bias, gdn_norm, w_out, pool_w, pool_scale,
              ffn_norm, w_ff1, w_ff2, final_norm):
    y_prompt = trunk(x_prompt, mix_norm, w_in, ssd_conv_w, ssd_conv_b, ssd_a_log, ssd_dt_bias, ssd_d,
                     ssd_norm, gdn_conv_w, gdn_a_log, gdn_dt_bias, gdn_norm, w_out, pool_w, pool_scale,
                     ffn_norm, w_ff1, w_ff2, final_norm)
    y_sample = trunk(x_sample, mix_norm, w_in, ssd_conv_w, ssd_conv_b, ssd_a_log, ssd_dt_bias, ssd_d,
                     ssd_norm, gdn_conv_w, gdn_a_log, gdn_dt_bias, gdn_norm, w_out, pool_w, pool_scale,
                     ffn_norm, w_ff1, w_ff2, final_norm)
    return (y_prompt, y_sample)
```

```python
import functools

import numpy as np
import jax
import jax.numpy as jnp
from jax import lax
from jax.experimental import pallas as pl
from jax.experimental.pallas import tpu as pltpu

F32 = jnp.float32
BF16 = jnp.bfloat16

D_MODEL = 2048
DEPTH = 2
SSD_D_INNER = 2048
SSD_HEAD_DIM = 64
SSD_HEADS = 32
SSD_GROUPS = 4
SSD_HEADS_PER_GROUP = 8
SSD_STATE = 128
SSD_CHUNK = 128
GDN_HEADS = 16
GDN_HEAD_K = 128
GDN_HEAD_V = 128
GDN_CHUNK = 64
CONV_W = 5
POOL_WINDOWS = (2, 4, 8, 16)
POOL_GROUP_DIM = 512
D_FF = 8192
EPS = 1e-6

HALO = 16
SSD_GROUP_COLS = 1280
SSD_CONV_COLS = 768
GDN_HB = 4
GDN_HEAD_COLS = 512
GDN_CONV_COLS = 384
GDN_TILE = 2 * GDN_CHUNK
SMALL_COLS = 128
SSD_PROJ_COLS = SSD_GROUPS * SSD_GROUP_COLS
GDN_PROJ_COLS = GDN_HEADS * GDN_HEAD_COLS
SMALLS_COLS = SSD_GROUPS * SMALL_COLS + (GDN_HEADS // GDN_HB) * SMALL_COLS
PROJ_DTYPE = F32

V7X_VMEM_BYTES = 64 * 1024 * 1024
VMEM_COMPILER_SCRATCH_BYTES = 12 * 1024 * 1024


def _vmem_limit(pipelined_bytes, scratch_bytes=0):
    need = 2 * pipelined_bytes + scratch_bytes + VMEM_COMPILER_SCRATCH_BYTES
    return int(min(need, V7X_VMEM_BYTES - 4 * 1024 * 1024))


def _nbytes(shape, dtype):
    return int(np.prod(shape)) * jnp.dtype(dtype).itemsize


def _dot(a, b):
    return jnp.dot(a, b, preferred_element_type=F32)


def _dot_nt(a, b):
    return lax.dot_general(a, b, (((1,), (1,)), ((), ())), preferred_element_type=F32)


def _split3(x):
    hi = x.astype(BF16)
    r = x - hi.astype(F32)
    mid = r.astype(BF16)
    lo = (r - mid.astype(F32)).astype(BF16)
    return hi, mid, lo


def _dot_sel_lhs(sel, x):
    hi, mid, lo = _split3(x)
    return (_dot(sel, hi) + _dot(sel, mid)) + _dot(sel, lo)


def _dot_sel_rhs(x, sel):
    hi, mid, lo = _split3(x)
    return (_dot(hi, sel) + _dot(mid, sel)) + _dot(lo, sel)


def _sigmoid(x):
    return 1.0 / (1.0 + jnp.exp(-x))


def _silu(x):
    return x * _sigmoid(x)


def _softplus(x):
    return jnp.maximum(x, 0.0) + jnp.log1p(jnp.exp(-jnp.abs(x)))


def _rms(x, w):
    return x * lax.rsqrt(jnp.mean(x * x, axis=-1, keepdims=True) + EPS) * w


def _rmsnorm_kernel(x_ref, g_ref, o_ref):
    o_ref[...] = _rms(x_ref[...], g_ref[...]).astype(o_ref.dtype)


def _rmsnorm(x, gain, out_dtype, tm=512):
    t, d = x.shape
    blocks = _nbytes((tm, d), F32) + _nbytes((tm, d), out_dtype)
    return pl.pallas_call(
        _rmsnorm_kernel,
        grid=(t // tm,),
        in_specs=[pl.BlockSpec((tm, d), lambda i: (i, 0)),
                  pl.BlockSpec((1, d), lambda i: (0, 0))],
        out_specs=pl.BlockSpec((tm, d), lambda i: (i, 0)),
        out_shape=jax.ShapeDtypeStruct((t, d), out_dtype),
        compiler_params=pltpu.CompilerParams(
            dimension_semantics=("arbitrary",), vmem_limit_bytes=_vmem_limit(blocks)),
        name="rmsnorm",
    )(x, gain.reshape(1, d))


def _matmul_kernel(*refs, n_pairs, relu2, has_res):
    a_refs = refs[:n_pairs]
    w_refs = refs[n_pairs:2 * n_pairs]
    res_ref = refs[2 * n_pairs] if has_res else None
    o_ref = refs[-1]
    acc = _dot(a_refs[0][...], w_refs[0][...])
    for p in range(1, n_pairs):
        acc = acc + _dot(a_refs[p][...], w_refs[p][...])
    if relu2:
        acc = jnp.square(jnp.maximum(acc, 0.0))
    if has_res:
        acc = acc + res_ref[...]
    o_ref[...] = acc.astype(o_ref.dtype)


def _matmul(a_list, w_list, out_dtype, tm, tn, relu2=False, res=None, name="matmul"):
    t = a_list[0].shape[0]
    n = w_list[0].shape[1]
    in_specs, blocks = [], 0
    for a in a_list:
        in_specs.append(pl.BlockSpec((tm, a.shape[1]), lambda i, j: (i, 0)))
        blocks += _nbytes((tm, a.shape[1]), a.dtype)
    for w in w_list:
        in_specs.append(pl.BlockSpec((w.shape[0], tn), lambda i, j: (0, j)))
        blocks += _nbytes((w.shape[0], tn), w.dtype)
    args = list(a_list) + list(w_list)
    if res is not None:
        in_specs.append(pl.BlockSpec((tm, tn), lambda i, j: (i, j)))
        blocks += _nbytes((tm, tn), res.dtype)
        args.append(res)
    blocks += _nbytes((tm, tn), out_dtype)
    kern = functools.partial(_matmul_kernel, n_pairs=len(a_list), relu2=relu2, has_res=res is not None)
    return pl.pallas_call(
        kern,
        grid=(t // tm, n // tn),
        in_specs=in_specs,
        out_specs=pl.BlockSpec((tm, tn), lambda i, j: (i, j)),
        out_shape=jax.ShapeDtypeStruct((t, n), out_dtype),
        compiler_params=pltpu.CompilerParams(
            dimension_semantics=("arbitrary", "arbitrary"),
            vmem_limit_bytes=_vmem_limit(blocks, _nbytes((tm, tn), F32))),
        name=name,
    )(*args)


def _ssd_kernel(*refs, rev, nc):
    if rev:
        (main, prv, nxt, sm, cw, cb, alog, dtb, dsk, nw, yprev, out, xw, st) = refs
    else:
        (main, prv, nxt, sm, cw, cb, alog, dtb, out, xw, st) = refs
    q = SSD_CHUNK
    c = pl.program_id(2)
    cc = (nc - 1 - c) if rev else c
    d = 1 if rev else 0

    @pl.when(c == 0)
    def _():
        st[...] = jnp.zeros_like(st)

    zero_h = jnp.zeros((HALO, SSD_CONV_COLS), F32)
    xw[0:HALO, :] = jnp.where(cc == 0, zero_h, prv[:, 512:SSD_GROUP_COLS].astype(F32))
    xw[HALO:HALO + q, :] = main[:, 512:SSD_GROUP_COLS].astype(F32)
    xw[HALO + q:HALO + q + HALO, :] = jnp.where(cc == nc - 1, zero_h, nxt[:, 512:SSD_GROUP_COLS].astype(F32))
    acc = cb[...]
    for k in range(CONV_W):
        acc = acc + cw[k:k + 1, :] * xw[pl.ds(HALO - CONV_W // 2 + k, q), :]
    xbc = _silu(acc)
    xs = xbc[:, 0:512]
    bm = xbc[:, 512:640]
    cm = xbc[:, 640:768]

    dtv = _softplus(sm[...] + dtb[...])
    ac = dtv * (-jnp.exp(alog[...]))
    rowi = lax.broadcasted_iota(jnp.int32, (q, q), 0)
    lane = lax.broadcasted_iota(jnp.int32, (q, q), 1)
    tri = (lane >= rowi) if rev else (lane <= rowi)
    tri_bf = jnp.where(tri, 1.0, 0.0).astype(BF16)
    cs = _dot_sel_lhs(tri_bf, ac)
    cs_t = cs.T
    tot = cs[0:1, :] if rev else cs[q - 1:q, :]
    ecs = jnp.exp(cs)
    edec = jnp.exp(tot - cs)

    er = lax.broadcasted_iota(jnp.int32, (SMALL_COLS, 512), 0)
    eq = lax.broadcasted_iota(jnp.int32, (SMALL_COLS, 512), 1)
    expand = jnp.where(er == d * 8 + lax.shift_right_logical(eq, 6), 1.0, 0.0).astype(BF16)
    ex = _dot_sel_rhs(jnp.concatenate([dtv, ecs, edec], axis=0), expand)
    dt_e = ex[0:q]
    ecs_e = ex[q:2 * q]
    edec_e = ex[2 * q:3 * q]

    xdt = xs * dt_e
    cm_bf = cm.astype(BF16)
    g = _dot_nt(cm_bf, bm.astype(BF16))
    s_old = st[...]
    new_state = _dot(bm.T.astype(BF16), (xdt * edec_e).astype(BF16))
    y_off = _dot(cm_bf, s_old.astype(BF16)) * ecs_e

    xdt_bf = xdt.astype(BF16)
    pieces = []
    for p in range(SSD_HEADS_PER_GROUP // 2):
        rhs = xdt_bf[:, p * 128:(p + 1) * 128]
        ys = []
        for t in range(2):
            ch = d * 8 + 2 * p + t
            seg = cs[:, ch:ch + 1] - cs_t[ch:ch + 1, :]
            lmat = jnp.exp(jnp.where(tri, seg, -jnp.inf))
            ys.append(_dot((g * lmat).astype(BF16), rhs))
        pieces.append(jnp.where(lane < SSD_HEAD_DIM, ys[0], ys[1]))
    y = jnp.concatenate(pieces, axis=1) + y_off

    chunk_decay = ecs_e[0:1, :] if rev else ecs_e[q - 1:q, :]
    st[...] = s_old * chunk_decay + new_state

    if rev:
        ytot = yprev[...] + y + dsk[...] * xs
        ytot = ytot * _silu(main[:, 0:512].astype(F32))
        out[...] = _rms(ytot, nw[...]).astype(out.dtype)
    else:
        out[...] = y


def _ssd_mixer(pbig, psm, prm):
    b, l, _ = pbig.shape
    nc = l // SSD_CHUNK
    hb = SSD_CHUNK // HALO
    nhb = l // HALO

    def cidx(c, rev):
        return (nc - 1 - c) if rev else c

    def specs(rev):
        return [
            pl.BlockSpec((None, SSD_CHUNK, SSD_GROUP_COLS), lambda bi, g, c: (bi, cidx(c, rev), g)),
            pl.BlockSpec((None, HALO, SSD_GROUP_COLS),
                         lambda bi, g, c: (bi, jnp.maximum(cidx(c, rev) * hb - 1, 0), g)),
            pl.BlockSpec((None, HALO, SSD_GROUP_COLS),
                         lambda bi, g, c: (bi, jnp.minimum((cidx(c, rev) + 1) * hb, nhb - 1), g)),
            pl.BlockSpec((None, SSD_CHUNK, SMALL_COLS), lambda bi, g, c: (bi, cidx(c, rev), g)),
            pl.BlockSpec((CONV_W, SSD_CONV_COLS), lambda bi, g, c: (0, g)),
            pl.BlockSpec((1, SSD_CONV_COLS), lambda bi, g, c: (0, g)),
            pl.BlockSpec((1, SMALL_COLS), lambda bi, g, c: (0, g)),
            pl.BlockSpec((1, SMALL_COLS), lambda bi, g, c: (0, g)),
        ]

    def yspec(rev):
        return pl.BlockSpec((None, SSD_CHUNK, 512), lambda bi, g, c: (bi, cidx(c, rev), g))

    scratch = [pltpu.VMEM((SSD_CHUNK + 2 * HALO, SSD_CONV_COLS), F32),
               pltpu.VMEM((SSD_STATE, 512), F32)]
    blocks = (_nbytes((SSD_CHUNK + 2 * HALO, SSD_GROUP_COLS), pbig.dtype)
              + _nbytes((SSD_CHUNK, SMALL_COLS), F32) + 2 * _nbytes((SSD_CHUNK, 512), F32))
    cparams = pltpu.CompilerParams(
        dimension_semantics=("arbitrary", "arbitrary", "arbitrary"),
        vmem_limit_bytes=_vmem_limit(blocks, _nbytes((SSD_CHUNK + 2 * HALO + SSD_STATE, 768), F32)))
    common = (pbig, pbig, pbig, psm, prm["ssd_cw"], prm["ssd_cb"], prm["ssd_alog"], prm["ssd_dtb"])
    y_fwd = pl.pallas_call(
        functools.partial(_ssd_kernel, rev=False, nc=nc),
        grid=(b, SSD_GROUPS, nc),
        in_specs=specs(False),
        out_specs=yspec(False),
        out_shape=jax.ShapeDtypeStruct((b, l, SSD_D_INNER), F32),
        scratch_shapes=scratch,
        compiler_params=cparams,
        name="ssd_fwd",
    )(*common)
    return pl.pallas_call(
        functools.partial(_ssd_kernel, rev=True, nc=nc),
        grid=(b, SSD_GROUPS, nc),
        in_specs=specs(True) + [pl.BlockSpec((1, 512), lambda bi, g, c: (0, g)),
                                pl.BlockSpec((1, 512), lambda bi, g, c: (0, g)),
                                yspec(True)],
        out_specs=yspec(True),
        out_shape=jax.ShapeDtypeStruct((b, l, SSD_D_INNER), BF16),
        scratch_shapes=scratch,
        compiler_params=cparams,
        name="ssd_bwd",
    )(*common, prm["ssd_dskip"], prm["ssd_norm"], y_fwd)


def _gdn_kernel(*refs, rev, nc):
    if rev:
        (main, prv, nxt, sm, cw, alog, dtb, nw, oprev, out, xw, st) = refs
    else:
        (main, prv, nxt, sm, cw, alog, dtb, out, xw, st) = refs
    n = GDN_TILE
    q = GDN_CHUNK
    c = pl.program_id(2)
    cc = (nc - 1 - c) if rev else c
    d = 1 if rev else 0

    @pl.when(c == 0)
    def _():
        st[...] = jnp.zeros_like(st)

    cols = GDN_HB * GDN_HEAD_COLS
    zero_h = jnp.zeros((HALO, cols), F32)
    xw[0:HALO, :] = jnp.where(cc == 0, zero_h, prv[...].astype(F32))
    xw[HALO:HALO + n, :] = main[...].astype(F32)
    xw[HALO + n:HALO + n + HALO, :] = jnp.where(cc == nc - 1, zero_h, nxt[...].astype(F32))

    smv = sm[...]
    beta_all = _sigmoid(smv)
    g_all = -jnp.exp(alog[...]) * _softplus(smv + dtb[...])
    rowi = lax.broadcasted_iota(jnp.int32, (n, n), 0)
    lane = lax.broadcasted_iota(jnp.int32, (n, n), 1)
    same = lax.shift_right_logical(rowi, 6) == lax.shift_right_logical(lane, 6)
    tri = same & ((lane >= rowi) if rev else (lane <= rowi))
    strict = same & ((lane > rowi) if rev else (lane < rowi))
    gc = _dot_sel_lhs(jnp.where(tri, 1.0, 0.0).astype(BF16), g_all)
    gc_t = gc.T
    first_last, second_last = (0, q) if rev else (q - 1, n - 1)
    gl = jnp.where(rowi < q, gc[first_last:first_last + 1, :], gc[second_last:second_last + 1, :])
    egc = jnp.exp(gc)
    ekd = jnp.exp(gl - gc)
    egl = jnp.exp(gl)
    eye = jnp.where(rowi == lane, 1.0, 0.0)
    zeros_q = jnp.zeros((q, GDN_HEAD_V), F32)

    outs = []
    for j in range(GDN_HB):
        h0 = j * GDN_HEAD_COLS
        acc = cw[0:1, j * GDN_CONV_COLS:(j + 1) * GDN_CONV_COLS] * xw[pl.ds(HALO - 2, n), h0:h0 + GDN_CONV_COLS]
        for k in range(1, CONV_W):
            acc = acc + (cw[k:k + 1, j * GDN_CONV_COLS:(j + 1) * GDN_CONV_COLS]
                         * xw[pl.ds(HALO - 2 + k, n), h0:h0 + GDN_CONV_COLS])
        qkv = _silu(acc)
        qh = qkv[:, 0:128]
        kh = qkv[:, 128:256]
        vh = qkv[:, 256:384]
        qh = qh * lax.rsqrt(jnp.sum(qh * qh, axis=-1, keepdims=True) + EPS) * (GDN_HEAD_K ** -0.5)
        kh = kh * lax.rsqrt(jnp.sum(kh * kh, axis=-1, keepdims=True) + EPS)
        lb = d * GDN_HB + j
        lg = 2 * GDN_HB + d * GDN_HB + j
        beta = beta_all[:, lb:lb + 1]
        dec = jnp.exp(jnp.where(tri, gc[:, lg:lg + 1] - gc_t[lg:lg + 1, :], -jnp.inf))
        kb = kh * beta
        vb = vh * beta
        k_bf = kh.astype(BF16)
        m = jnp.where(strict, _dot_nt(kb.astype(BF16), k_bf) * dec, 0.0)
        tinv = eye - m
        mk = m
        for _ in range(5):
            mk_bf = mk.astype(BF16)
            mk = _dot(mk_bf, mk_bf)
            tinv = tinv + _dot(tinv.astype(BF16), mk.astype(BF16))
        t_bf = tinv.astype(BF16)
        u = _dot(t_bf, vb.astype(BF16))
        w = _dot(t_bf, (kb * egc[:, lg:lg + 1]).astype(BF16))
        attn = jnp.where(tri, _dot_nt(qh.astype(BF16), k_bf) * dec, 0.0)
        kdec_t = (kh * ekd[:, lg:lg + 1]).T.astype(BF16)
        qdec = qh * egc[:, lg:lg + 1]

        s = st[j]
        vns = [None, None]
        oqs = [None, None]
        for ci in ((1, 0) if rev else (0, 1)):
            r0 = ci * q
            wq = jnp.concatenate([w[r0:r0 + q], qdec[r0:r0 + q]], axis=0).astype(BF16)
            res = _dot(wq, s.astype(BF16))
            vn = u[r0:r0 + q] - res[0:q]
            oqs[ci] = res[q:n]
            vnpad = jnp.concatenate([vn, zeros_q] if ci == 0 else [zeros_q, vn], axis=0)
            s = s * egl[r0:r0 + 1, lg:lg + 1] + _dot(kdec_t, vnpad.astype(BF16))
            vns[ci] = vn
        st[j] = s
        o = (jnp.concatenate(oqs, axis=0)
             + _dot(attn.astype(BF16), jnp.concatenate(vns, axis=0).astype(BF16)))
        if rev:
            ot = oprev[:, j * GDN_HEAD_V:(j + 1) * GDN_HEAD_V] + o
            ot = _rms(ot, nw[...])
            zh = xw[HALO:HALO + n, h0 + GDN_CONV_COLS:h0 + GDN_HEAD_COLS]
            outs.append((ot * _silu(zh)).astype(out.dtype))
        else:
            outs.append(o)
    out[...] = jnp.concatenate(outs, axis=1)


def _gdn_mixer(pbig, psm, prm):
    b, l, _ = pbig.shape
    nc = l // GDN_TILE
    hb = GDN_TILE // HALO
    nhb = l // HALO
    nblk = GDN_HEADS // GDN_HB
    cols = GDN_HB * GDN_HEAD_COLS
    sm0 = SSD_GROUPS

    def cidx(c, rev):
        return (nc - 1 - c) if rev else c

    def specs(rev):
        return [
            pl.BlockSpec((None, GDN_TILE, cols), lambda bi, h, c: (bi, cidx(c, rev), h)),
            pl.BlockSpec((None, HALO, cols),
                         lambda bi, h, c: (bi, jnp.maximum(cidx(c, rev) * hb - 1, 0), h)),
            pl.BlockSpec((None, HALO, cols),
                         lambda bi, h, c: (bi, jnp.minimum((cidx(c, rev) + 1) * hb, nhb - 1), h)),
            pl.BlockSpec((None, GDN_TILE, SMALL_COLS), lambda bi, h, c: (bi, cidx(c, rev), sm0 + h)),
            pl.BlockSpec((CONV_W, GDN_HB * GDN_CONV_COLS), lambda bi, h, c: (0, h)),
            pl.BlockSpec((1, SMALL_COLS), lambda bi, h, c: (0, h)),
            pl.BlockSpec((1, SMALL_COLS), lambda bi, h, c: (0, h)),
        ]

    def ospec(rev):
        return pl.BlockSpec((None, GDN_TILE, GDN_HB * GDN_HEAD_V), lambda bi, h, c: (bi, cidx(c, rev), h))

    scratch = [pltpu.VMEM((GDN_TILE + 2 * HALO, cols), F32),
               pltpu.VMEM((GDN_HB, GDN_HEAD_K, GDN_HEAD_V), F32)]
    blocks = (_nbytes((GDN_TILE + 2 * HALO, cols), pbig.dtype)
              + _nbytes((GDN_TILE, SMALL_COLS), F32) + 2 * _nbytes((GDN_TILE, GDN_HB * GDN_HEAD_V), F32))
    cparams = pltpu.CompilerParams(
        dimension_semantics=("arbitrary", "arbitrary", "arbitrary"),
        vmem_limit_bytes=_vmem_limit(blocks, _nbytes((GDN_TILE + 2 * HALO, cols), F32)
                                     + _nbytes((GDN_HB, GDN_HEAD_K, GDN_HEAD_V), F32)))
    common = (pbig, pbig, pbig, psm, prm["gdn_cw"], prm["gdn_alog"], prm["gdn_dtb"])
    o_fwd = pl.pallas_call(
        functools.partial(_gdn_kernel, rev=False, nc=nc),
        grid=(b, nblk, nc),
        in_specs=specs(False),
        out_specs=ospec(False),
        out_shape=jax.ShapeDtypeStruct((b, l, GDN_HEADS * GDN_HEAD_V), F32),
        scratch_shapes=scratch,
        compiler_params=cparams,
        name="gdn_fwd",
    )(*common)
    return pl.pallas_call(
        functools.partial(_gdn_kernel, rev=True, nc=nc),
        grid=(b, nblk, nc),
        in_specs=specs(True) + [pl.BlockSpec((1, GDN_HEAD_V), lambda bi, h, c: (0, 0)), ospec(True)],
        out_specs=ospec(True),
        out_shape=jax.ShapeDtypeStruct((b, l, GDN_HEADS * GDN_HEAD_V), BF16),
        scratch_shapes=scratch,
        compiler_params=cparams,
        name="gdn_bwd",
    )(*common, prm["gdn_norm"], o_fwd)


def _pool_kernel(x, xp, xn, gain, pw, ps, out, hw, *, nt, tm, seq):
    i = pl.program_id(1)
    zero_h = jnp.zeros((HALO, D_MODEL), F32)
    hw[0:HALO, :] = jnp.where(i == 0, zero_h, _rms(xp[...], gain[...]))
    hw[HALO:HALO + tm, :] = _rms(x[...], gain[...])
    hw[HALO + tm:HALO + tm + HALO, :] = jnp.where(i == nt - 1, zero_h, _rms(xn[...], gain[...]))
    t = i * tm + lax.broadcasted_iota(jnp.int32, (tm, 1), 0)
    for gi, win in enumerate(POOL_WINDOWS):
        lo_c = gi * POOL_GROUP_DIM
        hi_c = lo_c + POOL_GROUP_DIM
        half = win // 2
        s = hw[pl.ds(HALO - half, tm), lo_c:hi_c]
        for o in range(-half + 1, half):
            s = s + hw[pl.ds(HALO + o, tm), lo_c:hi_c]
        cnt = (jnp.minimum(t + half, seq) - jnp.maximum(t - half, 0)).astype(F32)
        p = s / cnt - hw[HALO:HALO + tm, lo_c:hi_c]
        y = _dot(p.astype(BF16), pw[gi]) * ps[:, lo_c:hi_c]
        out[:, lo_c:hi_c] = x[:, lo_c:hi_c] + y


def _pool_layer(x3, gain, pool_w_bf, pool_scale, tm=256):
    b, l, d = x3.shape
    nt = l // tm
    hb = tm // HALO
    nhb = l // HALO
    blocks = 2 * _nbytes((tm + 2 * HALO, d), F32) + _nbytes(pool_w_bf.shape, BF16)
    return pl.pallas_call(
        functools.partial(_pool_kernel, nt=nt, tm=tm, seq=l),
        grid=(b, nt),
        in_specs=[
            pl.BlockSpec((None, tm, d), lambda bi, i: (bi, i, 0)),
            pl.BlockSpec((None, HALO, d), lambda bi, i: (bi, jnp.maximum(i * hb - 1, 0), 0)),
            pl.BlockSpec((None, HALO, d), lambda bi, i: (bi, jnp.minimum((i + 1) * hb, nhb - 1), 0)),
            pl.BlockSpec((1, d), lambda bi, i: (0, 0)),
            pl.BlockSpec(pool_w_bf.shape, lambda bi, i: (0, 0, 0)),
            pl.BlockSpec((1, d), lambda bi, i: (0, 0)),
        ],
        out_specs=pl.BlockSpec((None, tm, d), lambda bi, i: (bi, i, 0)),
        out_shape=jax.ShapeDtypeStruct((b, l, d), F32),
        scratch_shapes=[pltpu.VMEM((tm + 2 * HALO, d), F32)],
        compiler_params=pltpu.CompilerParams(
            dimension_semantics=("arbitrary", "arbitrary"),
            vmem_limit_bytes=_vmem_limit(blocks, _nbytes((tm + 2 * HALO, d), F32))),
        name="pool_mixer",
    )(x3, x3, x3, gain.reshape(1, d), pool_w_bf, pool_scale.reshape(1, d))


def _in_proj_layout():
    z_a, x_a, b_a, c_a, dt_a = 0, 2048, 4096, 4608, 5120
    q_b, k_b, v_b, z_b, beta_b, a_b = 5184, 7232, 9280, 11328, 13376, 13408
    ssd_cols, gdn_cols = [], []
    for g in range(SSD_GROUPS):
        ssd_cols += list(range(z_a + g * 512, z_a + (g + 1) * 512))
        ssd_cols += list(range(x_a + g * 512, x_a + (g + 1) * 512))
        ssd_cols += list(range(b_a + g * 128, b_a + (g + 1) * 128))
        ssd_cols += list(range(c_a + g * 128, c_a + (g + 1) * 128))
    for h in range(GDN_HEADS):
        for base in (q_b, k_b, v_b, z_b):
            gdn_cols += list(range(base + h * 128, base + (h + 1) * 128))
    small_dst, small_src = [], []
    for g in range(SSD_GROUPS):
        for dr in range(2):
            for j in range(SSD_HEADS_PER_GROUP):
                small_dst.append(g * SMALL_COLS + dr * SSD_HEADS_PER_GROUP + j)
                small_src.append(dt_a + dr * SSD_HEADS + g * SSD_HEADS_PER_GROUP + j)
    for hbk in range(GDN_HEADS // GDN_HB):
        for kind, base in enumerate((beta_b, a_b)):
            for dr in range(2):
                for j in range(GDN_HB):
                    small_dst.append((SSD_GROUPS + hbk) * SMALL_COLS + (2 * kind + dr) * GDN_HB + j)
                    small_src.append(base + dr * GDN_HEADS + hbk * GDN_HB + j)
    ssd_conv = []
    for g in range(SSD_GROUPS):
        ssd_conv += list(range(g * 512, (g + 1) * 512))
        ssd_conv += list(range(2048 + g * 128, 2048 + (g + 1) * 128))
        ssd_conv += list(range(2560 + g * 128, 2560 + (g + 1) * 128))
    gdn_conv = []
    for h in range(GDN_HEADS):
        for base in (0, 2048, 4096):
            gdn_conv += list(range(base + h * 128, base + (h + 1) * 128))
    return (np.array(ssd_cols), np.array(gdn_cols), np.array(small_dst), np.array(small_src),
            np.array(ssd_conv), np.array(gdn_conv))


def _even_layer_params(j, w_in, ssd_conv_w, ssd_conv_b, ssd_a_log, ssd_dt_bias, ssd_d, ssd_norm,
                       gdn_conv_w, gdn_a_log, gdn_dt_bias, gdn_norm, w_out):
    ssd_cols, gdn_cols, small_dst, small_src, ssd_conv, gdn_conv = _in_proj_layout()
    w = w_in[j]
    w_small = jnp.zeros((D_MODEL, SMALLS_COLS), F32).at[:, small_dst].set(w[:, small_src])

    def ssd_lanes(v):
        v = v.reshape(2, SSD_GROUPS, SSD_HEADS_PER_GROUP).transpose(1, 0, 2).reshape(SSD_GROUPS, 16)
        return jnp.pad(v, ((0, 0), (0, SMALL_COLS - 16))).reshape(1, SSD_GROUPS * SMALL_COLS)

    def gdn_lanes(v):
        nb = GDN_HEADS // GDN_HB
        v = v.reshape(2, nb, GDN_HB).transpose(1, 0, 2).reshape(nb, 2 * GDN_HB)
        return jnp.pad(v, ((0, 0), (2 * GDN_HB, SMALL_COLS - 4 * GDN_HB))).reshape(1, nb * SMALL_COLS)

    return {
        "w_ssd": w[:, ssd_cols].astype(BF16),
        "w_gdn": w[:, gdn_cols].astype(BF16),
        "w_small": w_small.astype(BF16),
        "ssd_cw": ssd_conv_w[j][:, ssd_conv],
        "ssd_cb": ssd_conv_b[j][ssd_conv].reshape(1, -1),
        "ssd_alog": ssd_lanes(ssd_a_log[j]),
        "ssd_dtb": ssd_lanes(ssd_dt_bias[j]),
        "ssd_dskip": jnp.repeat(ssd_d[j], SSD_HEAD_DIM).reshape(1, SSD_D_INNER),
        "ssd_norm": ssd_norm[j].reshape(1, SSD_D_INNER),
        "gdn_cw": gdn_conv_w[j][:, gdn_conv],
        "gdn_alog": gdn_lanes(gdn_a_log[j]),
        "gdn_dtb": gdn_lanes(gdn_dt_bias[j]),
        "gdn_norm": gdn_norm[j].reshape(1, GDN_HEAD_V),
        "w_out_a": w_out[j][:SSD_D_INNER].astype(BF16),
        "w_out_b": w_out[j][SSD_D_INNER:].astype(BF16),
    }


def _mlp(x2, gain, w1_bf, w2_bf):
    h = _rmsnorm(x2, gain, BF16)
    hid = _matmul([h], [w1_bf], BF16, tm=1024, tn=1024, relu2=True, name="ffn_up")
    return _matmul([hid], [w2_bf], F32, tm=512, tn=512, res=x2, name="ffn_down")


def _trunk(x, mix_norm, even_params, pool_w_bf, pool_scale, ffn_norm, w_ff1_bf, w_ff2_bf, final_norm):
    b, l, d = x.shape
    t = b * l
    x2 = x.reshape(t, d)
    for i in range(DEPTH):
        j = i // 2
        if i % 2 == 0:
            prm = even_params[j]
            h = _rmsnorm(x2, mix_norm[i], BF16)
            pssd = _matmul([h], [prm["w_ssd"]], PROJ_DTYPE, tm=1024, tn=1024, name="in_proj_ssd")
            pgdn = _matmul([h], [prm["w_gdn"]], PROJ_DTYPE, tm=1024, tn=1024, name="in_proj_gdn")
            psm = _matmul([h], [prm["w_small"]], F32, tm=1024, tn=SMALLS_COLS, name="in_proj_small")
            psm = psm.reshape(b, l, SMALLS_COLS)
            ya = _ssd_mixer(pssd.reshape(b, l, SSD_PROJ_COLS), psm, prm).reshape(t, SSD_D_INNER)
            yb = _gdn_mixer(pgdn.reshape(b, l, GDN_PROJ_COLS), psm, prm).reshape(t, GDN_HEADS * GDN_HEAD_V)
            x2 = _matmul([ya, yb], [prm["w_out_a"], prm["w_out_b"]], F32, tm=512, tn=1024, res=x2,
                         name="out_proj")
        else:
            x2 = _pool_layer(x2.reshape(b, l, d), mix_norm[i], pool_w_bf[j], pool_scale[j]).reshape(t, d)
        x2 = _mlp(x2, ffn_norm[i], w_ff1_bf[i], w_ff2_bf[i])
    return _rmsnorm(x2, final_norm, F32).reshape(b, l, d)


def kernel(x_prompt, x_sample, mix_norm, w_in, ssd_conv_w, ssd_conv_b, ssd_a_log, ssd_dt_bias, ssd_d, ssd_norm, gdn_conv_w, gdn_a_log, gdn_dt_bias, gdn_norm, w_out, pool_w, pool_scale, ffn_norm, w_ff1, w_ff2, final_norm):
    even_params = [
        _even_layer_params(j, w_in, ssd_conv_w, ssd_conv_b, ssd_a_log, ssd_dt_bias, ssd_d, ssd_norm,
                           gdn_conv_w, gdn_a_log, gdn_dt_bias, gdn_norm, w_out)
        for j in range(w_in.shape[0])
    ]
    pool_w_bf = pool_w.astype(BF16)
    w_ff1_bf = w_ff1.astype(BF16)
    w_ff2_bf = w_ff2.astype(BF16)
    args = (mix_norm, even_params, pool_w_bf, pool_scale, ffn_norm, w_ff1_bf, w_ff2_bf, final_norm)
    return (_trunk(x_prompt, *args), _trunk(x_sample, *args))
```

```python
import functools

import numpy as np
import jax
import jax.numpy as jnp
from jax import lax
from jax.experimental import pallas as pl
from jax.experimental.pallas import tpu as pltpu

F32 = jnp.float32
BF16 = jnp.bfloat16

D_MODEL = 2048
DEPTH = 2
SSD_D_INNER = 2048
SSD_HEAD_DIM = 64
SSD_HEADS = 32
SSD_GROUPS = 4
SSD_HEADS_PER_GROUP = 8
SSD_STATE = 128
SSD_CHUNK = 128
GDN_HEADS = 16
GDN_HEAD_K = 128
GDN_HEAD_V = 128
GDN_CHUNK = 64
CONV_W = 5
POOL_WINDOWS = (2, 4, 8, 16)
POOL_GROUP_DIM = 512
D_FF = 8192
EPS = 1e-6

LANES = 128
HALO = 16
SSD_GROUP_COLS = 1280
SSD_CONV_COLS = 768
GDN_HB = 16
GDN_SKEW = 1
GDN_HEAD_COLS = 512
GDN_CONV_COLS = 384
GDN_TILE = 2 * GDN_CHUNK
SMALL_COLS = 128
SSD_PROJ_COLS = SSD_GROUPS * SSD_GROUP_COLS
GDN_PROJ_COLS = GDN_HEADS * GDN_HEAD_COLS
SMALLS_COLS = SSD_GROUPS * SMALL_COLS + (GDN_HEADS // GDN_HB) * SMALL_COLS
PROJ_DTYPE = BF16
SSD_GROUP_SLABS = SSD_GROUP_COLS // LANES
SSD_CONV_SLABS = SSD_CONV_COLS // LANES
GDN_HEAD_SLABS = GDN_HEAD_COLS // LANES

V7X_VMEM_BYTES = 64 * 1024 * 1024
VMEM_COMPILER_SCRATCH_BYTES = 12 * 1024 * 1024


def _vmem_limit(pipelined_bytes, scratch_bytes=0):
    need = 2 * pipelined_bytes + scratch_bytes + VMEM_COMPILER_SCRATCH_BYTES
    return int(min(need, V7X_VMEM_BYTES - 4 * 1024 * 1024))


def _nbytes(shape, dtype):
    return int(np.prod(shape)) * jnp.dtype(dtype).itemsize


def _dot(a, b):
    return jnp.dot(a, b, preferred_element_type=F32)


def _dot_nt(a, b):
    return lax.dot_general(a, b, (((1,), (1,)), ((), ())), preferred_element_type=F32)


def _split3(x):
    hi = x.astype(BF16)
    r = x - hi.astype(F32)
    mid = r.astype(BF16)
    lo = (r - mid.astype(F32)).astype(BF16)
    return hi, mid, lo


def _dot_sel_lhs(sel, x):
    hi, mid, lo = _split3(x)
    return (_dot(sel, hi) + _dot(sel, mid)) + _dot(sel, lo)


def _dot_sel_rhs(x, sel):
    hi, mid, lo = _split3(x)
    return (_dot(hi, sel) + _dot(mid, sel)) + _dot(lo, sel)


def _sigmoid(x):
    return 1.0 / (1.0 + jnp.exp(-x))


def _silu(x):
    return x * _sigmoid(x)


def _softplus(x):
    return jnp.maximum(x, 0.0) + jnp.log1p(jnp.exp(-jnp.abs(x)))


def _rms(x, w):
    return x * lax.rsqrt(jnp.mean(x * x, axis=-1, keepdims=True) + EPS) * w


def _rmsnorm_kernel(x_ref, g_ref, o_ref):
    o_ref[...] = _rms(x_ref[...], g_ref[...]).astype(o_ref.dtype)


def _rmsnorm(x, gain, out_dtype, tm=512):
    t, d = x.shape
    blocks = _nbytes((tm, d), F32) + _nbytes((tm, d), out_dtype)
    return pl.pallas_call(
        _rmsnorm_kernel,
        grid=(t // tm,),
        in_specs=[pl.BlockSpec((tm, d), lambda i: (i, 0)),
                  pl.BlockSpec((1, d), lambda i: (0, 0))],
        out_specs=pl.BlockSpec((tm, d), lambda i: (i, 0)),
        out_shape=jax.ShapeDtypeStruct((t, d), out_dtype),
        compiler_params=pltpu.CompilerParams(
            dimension_semantics=("arbitrary",), vmem_limit_bytes=_vmem_limit(blocks)),
        name="rmsnorm",
    )(x, gain.reshape(1, d))


def _matmul_kernel(*refs, n_pairs, relu2, has_res, slab_out):
    a_refs = refs[:n_pairs]
    w_refs = refs[n_pairs:2 * n_pairs]
    res_ref = refs[2 * n_pairs] if has_res else None
    o_ref = refs[-1]
    acc = _dot(a_refs[0][...], w_refs[0][...])
    for p in range(1, n_pairs):
        acc = acc + _dot(a_refs[p][...], w_refs[p][...])
    if relu2:
        acc = jnp.square(jnp.maximum(acc, 0.0))
    if has_res:
        acc = acc + res_ref[...]
    if slab_out:
        for s in range(o_ref.shape[0]):
            o_ref[s] = acc[:, s * LANES:(s + 1) * LANES].astype(o_ref.dtype)
    else:
        o_ref[...] = acc.astype(o_ref.dtype)


def _matmul(a_list, w_list, out_dtype, tm, tn, relu2=False, res=None, slab_out=False, name="matmul"):
    t = a_list[0].shape[0]
    n = w_list[0].shape[1]
    in_specs, blocks = [], 0
    for a in a_list:
        in_specs.append(pl.BlockSpec((tm, a.shape[1]), lambda i, j: (i, 0)))
        blocks += _nbytes((tm, a.shape[1]), a.dtype)
    for w in w_list:
        in_specs.append(pl.BlockSpec((w.shape[0], tn), lambda i, j: (0, j)))
        blocks += _nbytes((w.shape[0], tn), w.dtype)
    args = list(a_list) + list(w_list)
    if res is not None:
        in_specs.append(pl.BlockSpec((tm, tn), lambda i, j: (i, j)))
        blocks += _nbytes((tm, tn), res.dtype)
        args.append(res)
    blocks += _nbytes((tm, tn), out_dtype)
    kern = functools.partial(_matmul_kernel, n_pairs=len(a_list), relu2=relu2, has_res=res is not None,
                             slab_out=slab_out)
    if slab_out:
        out_spec = pl.BlockSpec((tn // LANES, tm, LANES), lambda i, j: (j, i, 0))
        out_shape = jax.ShapeDtypeStruct((n // LANES, t, LANES), out_dtype)
    else:
        out_spec = pl.BlockSpec((tm, tn), lambda i, j: (i, j))
        out_shape = jax.ShapeDtypeStruct((t, n), out_dtype)
    return pl.pallas_call(
        kern,
        grid=(t // tm, n // tn),
        in_specs=in_specs,
        out_specs=out_spec,
        out_shape=out_shape,
        compiler_params=pltpu.CompilerParams(
            dimension_semantics=("arbitrary", "arbitrary"),
            vmem_limit_bytes=_vmem_limit(blocks, _nbytes((tm, tn), F32))),
        name=name,
    )(*args)


def _ssd_kernel(*refs, rev, nc):
    if rev:
        (main, prv, nxt, sm, cw, cb, alog, dtb, dsk, nw, yprev, out, xw, st) = refs
    else:
        (main, prv, nxt, sm, cw, cb, alog, dtb, out, xw, st) = refs
    q = SSD_CHUNK
    c = pl.program_id(2)
    cc = (nc - 1 - c) if rev else c
    d = 1 if rev else 0

    @pl.when(c == 0)
    def _():
        st[...] = jnp.zeros_like(st)

    z0 = SSD_GROUP_SLABS - SSD_CONV_SLABS
    zero_h = jnp.zeros((SSD_CONV_SLABS, HALO, LANES), F32)
    xw[:, 0:HALO, :] = jnp.where(cc == 0, zero_h, prv[z0:SSD_GROUP_SLABS].astype(F32))
    xw[:, HALO:HALO + q, :] = main[z0:SSD_GROUP_SLABS].astype(F32)
    xw[:, HALO + q:HALO + q + HALO, :] = jnp.where(cc == nc - 1, zero_h, nxt[z0:SSD_GROUP_SLABS].astype(F32))
    conv = []
    for s in range(SSD_CONV_SLABS):
        acc = cb[:, s * LANES:(s + 1) * LANES]
        for k in range(CONV_W):
            acc = acc + cw[k:k + 1, s * LANES:(s + 1) * LANES] * xw[s, pl.ds(HALO - CONV_W // 2 + k, q), :]
        conv.append(_silu(acc))
    xs = jnp.concatenate(conv[0:4], axis=1)
    bm = conv[4]
    cm = conv[5]

    dtv = _softplus(sm[...] + dtb[...])
    ac = dtv * (-jnp.exp(alog[...]))
    rowi = lax.broadcasted_iota(jnp.int32, (q, q), 0)
    lane = lax.broadcasted_iota(jnp.int32, (q, q), 1)
    tri = (lane >= rowi) if rev else (lane <= rowi)
    tri_bf = jnp.where(tri, 1.0, 0.0).astype(BF16)
    cs = _dot_sel_lhs(tri_bf, ac)
    cs_t = cs.T
    tot = cs[0:1, :] if rev else cs[q - 1:q, :]
    ecs = jnp.exp(cs)
    edec = jnp.exp(tot - cs)

    er = lax.broadcasted_iota(jnp.int32, (SMALL_COLS, 512), 0)
    eq = lax.broadcasted_iota(jnp.int32, (SMALL_COLS, 512), 1)
    expand = jnp.where(er == d * 8 + lax.shift_right_logical(eq, 6), 1.0, 0.0).astype(BF16)
    ex = _dot_sel_rhs(jnp.concatenate([dtv, ecs, edec], axis=0), expand)
    dt_e = ex[0:q]
    ecs_e = ex[q:2 * q]
    edec_e = ex[2 * q:3 * q]

    xdt = xs * dt_e
    cm_bf = cm.astype(BF16)
    g = _dot_nt(cm_bf, bm.astype(BF16))
    s_old = st[...]
    new_state = _dot(bm.T.astype(BF16), (xdt * edec_e).astype(BF16))
    y_off = _dot(cm_bf, s_old.astype(BF16)) * ecs_e

    xdt_bf = xdt.astype(BF16)
    pieces = []
    for p in range(SSD_HEADS_PER_GROUP // 2):
        rhs = xdt_bf[:, p * 128:(p + 1) * 128]
        ys = []
        for t in range(2):
            ch = d * 8 + 2 * p + t
            seg = cs[:, ch:ch + 1] - cs_t[ch:ch + 1, :]
            lmat = jnp.exp(jnp.where(tri, seg, -jnp.inf))
            ys.append(_dot((g * lmat).astype(BF16), rhs))
        pieces.append(jnp.where(lane < SSD_HEAD_DIM, ys[0], ys[1]))
    y = jnp.concatenate(pieces, axis=1) + y_off

    chunk_decay = ecs_e[0:1, :] if rev else ecs_e[q - 1:q, :]
    st[...] = s_old * chunk_decay + new_state

    if rev:
        ytot = yprev[...] + y + dsk[...] * xs
        zg = jnp.concatenate([main[s].astype(F32) for s in range(z0)], axis=1)
        ytot = ytot * _silu(zg)
        out[...] = _rms(ytot, nw[...]).astype(out.dtype)
    else:
        out[...] = y


def _ssd_mixer(pbig, psm, prm):
    _, b, l, _ = pbig.shape
    nc = l // SSD_CHUNK
    hb = SSD_CHUNK // HALO
    nhb = l // HALO

    def cidx(c, rev):
        return (nc - 1 - c) if rev else c

    def specs(rev):
        return [
            pl.BlockSpec((SSD_GROUP_SLABS, None, SSD_CHUNK, LANES), lambda bi, g, c: (g, bi, cidx(c, rev), 0)),
            pl.BlockSpec((SSD_GROUP_SLABS, None, HALO, LANES),
                         lambda bi, g, c: (g, bi, jnp.maximum(cidx(c, rev) * hb - 1, 0), 0)),
            pl.BlockSpec((SSD_GROUP_SLABS, None, HALO, LANES),
                         lambda bi, g, c: (g, bi, jnp.minimum((cidx(c, rev) + 1) * hb, nhb - 1), 0)),
            pl.BlockSpec((None, SSD_CHUNK, SMALL_COLS), lambda bi, g, c: (bi, cidx(c, rev), g)),
            pl.BlockSpec((CONV_W, SSD_CONV_COLS), lambda bi, g, c: (0, g)),
            pl.BlockSpec((1, SSD_CONV_COLS), lambda bi, g, c: (0, g)),
            pl.BlockSpec((1, SMALL_COLS), lambda bi, g, c: (0, g)),
            pl.BlockSpec((1, SMALL_COLS), lambda bi, g, c: (0, g)),
        ]

    def yspec(rev):
        return pl.BlockSpec((None, SSD_CHUNK, 512), lambda bi, g, c: (bi, cidx(c, rev), g))

    scratch = [pltpu.VMEM((SSD_CONV_SLABS, SSD_CHUNK + 2 * HALO, LANES), F32),
               pltpu.VMEM((SSD_STATE, 512), F32)]
    blocks = (_nbytes((SSD_CHUNK + 2 * HALO, SSD_GROUP_COLS), pbig.dtype)
              + _nbytes((SSD_CHUNK, SMALL_COLS), F32) + 2 * _nbytes((SSD_CHUNK, 512), F32))
    cparams = pltpu.CompilerParams(
        dimension_semantics=("arbitrary", "arbitrary", "arbitrary"),
        vmem_limit_bytes=_vmem_limit(blocks, _nbytes((SSD_CHUNK + 2 * HALO + SSD_STATE, 768), F32)))
    common = (pbig, pbig, pbig, psm, prm["ssd_cw"], prm["ssd_cb"], prm["ssd_alog"], prm["ssd_dtb"])
    y_fwd = pl.pallas_call(
        functools.partial(_ssd_kernel, rev=False, nc=nc),
        grid=(b, SSD_GROUPS, nc),
        in_specs=specs(False),
        out_specs=yspec(False),
        out_shape=jax.ShapeDtypeStruct((b, l, SSD_D_INNER), F32),
        scratch_shapes=scratch,
        compiler_params=cparams,
        name="ssd_fwd",
    )(*common)
    return pl.pallas_call(
        functools.partial(_ssd_kernel, rev=True, nc=nc),
        grid=(b, SSD_GROUPS, nc),
        in_specs=specs(True) + [pl.BlockSpec((1, 512), lambda bi, g, c: (0, g)),
                                pl.BlockSpec((1, 512), lambda bi, g, c: (0, g)),
                                yspec(True)],
        out_specs=yspec(True),
        out_shape=jax.ShapeDtypeStruct((b, l, SSD_D_INNER), BF16),
        scratch_shapes=scratch,
        compiler_params=cparams,
        name="ssd_bwd",
    )(*common, prm["ssd_dskip"], prm["ssd_norm"], y_fwd)


def _gdn_kernel(*refs, rev, nc):
    if rev:
        (main, prv, nxt, sm, cw, alog, dtb, nw, oprev, out, xw, st) = refs
    else:
        (main, prv, nxt, sm, cw, alog, dtb, out, xw, st) = refs
    n = GDN_TILE
    q = GDN_CHUNK
    c = pl.program_id(2)
    cc = (nc - 1 - c) if rev else c
    d = 1 if rev else 0

    @pl.when(c == 0)
    def _():
        st[...] = jnp.zeros_like(st)

    zero_h = jnp.zeros((GDN_HB * GDN_HEAD_SLABS, HALO, LANES), F32)
    xw[:, 0:HALO, :] = jnp.where(cc == 0, zero_h, prv[...].astype(F32))
    xw[:, HALO:HALO + n, :] = main[...].astype(F32)
    xw[:, HALO + n:HALO + n + HALO, :] = jnp.where(cc == nc - 1, zero_h, nxt[...].astype(F32))

    smv = sm[...]
    beta_all = _sigmoid(smv)
    g_all = -jnp.exp(alog[...]) * _softplus(smv + dtb[...])
    rowi = lax.broadcasted_iota(jnp.int32, (n, n), 0)
    lane = lax.broadcasted_iota(jnp.int32, (n, n), 1)
    same = lax.shift_right_logical(rowi, 6) == lax.shift_right_logical(lane, 6)
    tri = same & ((lane >= rowi) if rev else (lane <= rowi))
    strict = same & ((lane > rowi) if rev else (lane < rowi))
    gc = _dot_sel_lhs(jnp.where(tri, 1.0, 0.0).astype(BF16), g_all)
    gc_t = gc.T
    first_last, second_last = (0, q) if rev else (q - 1, n - 1)
    gl = jnp.where(rowi < q, gc[first_last:first_last + 1, :], gc[second_last:second_last + 1, :])
    egc = jnp.exp(gc)
    ekd = jnp.exp(gl - gc)
    egl = jnp.exp(gl)
    eye = jnp.where(rowi == lane, 1.0, 0.0)
    zeros_q = jnp.zeros((q, GDN_HEAD_V), F32)

    def head_stages(j):
        s0 = j * GDN_HEAD_SLABS
        conv = []
        for s in range(GDN_CONV_COLS // LANES):
            c0 = j * GDN_CONV_COLS + s * LANES
            acc = cw[0:1, c0:c0 + LANES] * xw[s0 + s, pl.ds(HALO - CONV_W // 2, n), :]
            for k in range(1, CONV_W):
                acc = acc + cw[k:k + 1, c0:c0 + LANES] * xw[s0 + s, pl.ds(HALO - CONV_W // 2 + k, n), :]
            conv.append(_silu(acc))
        qh, kh, vh = conv
        qh = qh * (lax.rsqrt(jnp.sum(qh * qh, axis=-1, keepdims=True) + EPS) * (GDN_HEAD_K ** -0.5))
        kh = kh * lax.rsqrt(jnp.sum(kh * kh, axis=-1, keepdims=True) + EPS)
        lb = d * GDN_HB + j
        lg = 2 * GDN_HB + d * GDN_HB + j
        beta = beta_all[:, lb:lb + 1]
        egc_col = egc[:, lg:lg + 1]
        dec = jnp.exp(jnp.where(tri, gc[:, lg:lg + 1] - gc_t[lg:lg + 1, :], -jnp.inf))
        kb = kh * beta
        kq = _dot_nt(jnp.concatenate([kb, qh], axis=0).astype(BF16), kh.astype(BF16))
        m = jnp.where(strict, kq[0:n] * dec, 0.0)
        attn_bf = jnp.where(tri, kq[n:2 * n] * dec, 0.0).astype(BF16)
        tinv = eye - m
        mk_bf = m.astype(BF16)
        for _ in range(5):
            yield
            mk_bf = _dot(mk_bf, mk_bf).astype(BF16)
            yield
            tinv = tinv + _dot(tinv.astype(BF16), mk_bf)
        yield
        uw = _dot(tinv.astype(BF16),
                  jnp.concatenate([(vh * beta).astype(BF16), (kb * egc_col).astype(BF16)], axis=1))
        u = uw[:, 0:GDN_HEAD_V]
        w = uw[:, GDN_HEAD_V:2 * GDN_HEAD_V]
        kdec_t = (kh * ekd[:, lg:lg + 1]).T.astype(BF16)
        qdec = qh * egc_col
        s = st[j]
        vns = [None, None]
        oqs = [None, None]
        for ci in ((1, 0) if rev else (0, 1)):
            r0 = ci * q
            yield
            res = _dot(jnp.concatenate([w[r0:r0 + q], qdec[r0:r0 + q]], axis=0).astype(BF16), s.astype(BF16))
            vns[ci] = u[r0:r0 + q] - res[0:q]
            oqs[ci] = res[q:n]
            vnpad = jnp.concatenate([vns[ci], zeros_q] if ci == 0 else [zeros_q, vns[ci]], axis=0)
            yield
            s = s * egl[r0:r0 + 1, lg:lg + 1] + _dot(kdec_t, vnpad.astype(BF16))
        st[j] = s
        yield
        o = jnp.concatenate(oqs, axis=0) + _dot(attn_bf, jnp.concatenate(vns, axis=0).astype(BF16))
        if rev:
            ot = _rms(oprev[:, j * GDN_HEAD_V:(j + 1) * GDN_HEAD_V] + o, nw[...])
            zh = main[s0 + GDN_HEAD_SLABS - 1].astype(F32)
            out[:, j * GDN_HEAD_V:(j + 1) * GDN_HEAD_V] = (ot * _silu(zh)).astype(out.dtype)
        else:
            out[:, j * GDN_HEAD_V:(j + 1) * GDN_HEAD_V] = o

    pending = [head_stages(j) for j in range(GDN_HB)]
    active = []
    tick = 0
    while pending or active:
        if pending and tick % GDN_SKEW == 0:
            active.append(pending.pop(0))
        for gen in list(active):
            if next(gen, StopIteration) is StopIteration:
                active.remove(gen)
        tick += 1


def _gdn_mixer(pbig, psm, prm):
    _, b, l, _ = pbig.shape
    nc = l // GDN_TILE
    hb = GDN_TILE // HALO
    nhb = l // HALO
    nblk = GDN_HEADS // GDN_HB
    slabs = GDN_HB * GDN_HEAD_SLABS
    sm0 = SSD_GROUPS

    def cidx(c, rev):
        return (nc - 1 - c) if rev else c

    def specs(rev):
        return [
            pl.BlockSpec((slabs, None, GDN_TILE, LANES), lambda bi, h, c: (h, bi, cidx(c, rev), 0)),
            pl.BlockSpec((slabs, None, HALO, LANES),
                         lambda bi, h, c: (h, bi, jnp.maximum(cidx(c, rev) * hb - 1, 0), 0)),
            pl.BlockSpec((slabs, None, HALO, LANES),
                         lambda bi, h, c: (h, bi, jnp.minimum((cidx(c, rev) + 1) * hb, nhb - 1), 0)),
            pl.BlockSpec((None, GDN_TILE, SMALL_COLS), lambda bi, h, c: (bi, cidx(c, rev), sm0 + h)),
            pl.BlockSpec((CONV_W, GDN_HB * GDN_CONV_COLS), lambda bi, h, c: (0, h)),
            pl.BlockSpec((1, SMALL_COLS), lambda bi, h, c: (0, h)),
            pl.BlockSpec((1, SMALL_COLS), lambda bi, h, c: (0, h)),
        ]

    def ospec(rev):
        return pl.BlockSpec((None, GDN_TILE, GDN_HB * GDN_HEAD_V), lambda bi, h, c: (bi, cidx(c, rev), h))

    window = (slabs, GDN_TILE + 2 * HALO, LANES)
    scratch = [pltpu.VMEM(window, F32),
               pltpu.VMEM((GDN_HB, GDN_HEAD_K, GDN_HEAD_V), F32)]
    blocks = (_nbytes(window, pbig.dtype)
              + _nbytes((GDN_TILE, SMALL_COLS), F32) + 2 * _nbytes((GDN_TILE, GDN_HB * GDN_HEAD_V), F32))
    cparams = pltpu.CompilerParams(
        dimension_semantics=("arbitrary", "arbitrary", "arbitrary"),
        vmem_limit_bytes=_vmem_limit(blocks, _nbytes(window, F32)
                                     + _nbytes((GDN_HB, GDN_HEAD_K, GDN_HEAD_V), F32)))
    common = (pbig, pbig, pbig, psm, prm["gdn_cw"], prm["gdn_alog"], prm["gdn_dtb"])
    o_fwd = pl.pallas_call(
        functools.partial(_gdn_kernel, rev=False, nc=nc),
        grid=(b, nblk, nc),
        in_specs=specs(False),
        out_specs=ospec(False),
        out_shape=jax.ShapeDtypeStruct((b, l, GDN_HEADS * GDN_HEAD_V), F32),
        scratch_shapes=scratch,
        compiler_params=cparams,
        name="gdn_fwd",
    )(*common)
    return pl.pallas_call(
        functools.partial(_gdn_kernel, rev=True, nc=nc),
        grid=(b, nblk, nc),
        in_specs=specs(True) + [pl.BlockSpec((1, GDN_HEAD_V), lambda bi, h, c: (0, 0)), ospec(True)],
        out_specs=ospec(True),
        out_shape=jax.ShapeDtypeStruct((b, l, GDN_HEADS * GDN_HEAD_V), BF16),
        scratch_shapes=scratch,
        compiler_params=cparams,
        name="gdn_bwd",
    )(*common, prm["gdn_norm"], o_fwd)


def _pool_kernel(x, xp, xn, gain, pw, ps, out, hw, *, nt, tm, seq):
    i = pl.program_id(1)
    zero_h = jnp.zeros((HALO, D_MODEL), F32)
    hw[0:HALO, :] = jnp.where(i == 0, zero_h, _rms(xp[...], gain[...]))
    hw[HALO:HALO + tm, :] = _rms(x[...], gain[...])
    hw[HALO + tm:HALO + tm + HALO, :] = jnp.where(i == nt - 1, zero_h, _rms(xn[...], gain[...]))
    t = i * tm + lax.broadcasted_iota(jnp.int32, (tm, 1), 0)
    for gi, win in enumerate(POOL_WINDOWS):
        lo_c = gi * POOL_GROUP_DIM
        hi_c = lo_c + POOL_GROUP_DIM
        half = win // 2
        s = hw[pl.ds(HALO - half, tm), lo_c:hi_c]
        for o in range(-half + 1, half):
            s = s + hw[pl.ds(HALO + o, tm), lo_c:hi_c]
        cnt = (jnp.minimum(t + half, seq) - jnp.maximum(t - half, 0)).astype(F32)
        p = s / cnt - hw[HALO:HALO + tm, lo_c:hi_c]
        y = _dot(p.astype(BF16), pw[gi]) * ps[:, lo_c:hi_c]
        out[:, lo_c:hi_c] = x[:, lo_c:hi_c] + y


def _pool_layer(x3, gain, pool_w_bf, pool_scale, tm=256):
    b, l, d = x3.shape
    nt = l // tm
    hb = tm // HALO
    nhb = l // HALO
    blocks = 2 * _nbytes((tm + 2 * HALO, d), F32) + _nbytes(pool_w_bf.shape, BF16)
    return pl.pallas_call(
        functools.partial(_pool_kernel, nt=nt, tm=tm, seq=l),
        grid=(b, nt),
        in_specs=[
            pl.BlockSpec((None, tm, d), lambda bi, i: (bi, i, 0)),
            pl.BlockSpec((None, HALO, d), lambda bi, i: (bi, jnp.maximum(i * hb - 1, 0), 0)),
            pl.BlockSpec((None, HALO, d), lambda bi, i: (bi, jnp.minimum((i + 1) * hb, nhb - 1), 0)),
            pl.BlockSpec((1, d), lambda bi, i: (0, 0)),
            pl.BlockSpec(pool_w_bf.shape, lambda bi, i: (0, 0, 0)),
            pl.BlockSpec((1, d), lambda bi, i: (0, 0)),
        ],
        out_specs=pl.BlockSpec((None, tm, d), lambda bi, i: (bi, i, 0)),
        out_shape=jax.ShapeDtypeStruct((b, l, d), F32),
        scratch_shapes=[pltpu.VMEM((tm + 2 * HALO, d), F32)],
        compiler_params=pltpu.CompilerParams(
            dimension_semantics=("arbitrary", "arbitrary"),
            vmem_limit_bytes=_vmem_limit(blocks, _nbytes((tm + 2 * HALO, d), F32))),
        name="pool_mixer",
    )(x3, x3, x3, gain.reshape(1, d), pool_w_bf, pool_scale.reshape(1, d))


def _in_proj_layout():
    z_a, x_a, b_a, c_a, dt_a = 0, 2048, 4096, 4608, 5120
    q_b, k_b, v_b, z_b, beta_b, a_b = 5184, 7232, 9280, 11328, 13376, 13408
    ssd_cols, gdn_cols = [], []
    for g in range(SSD_GROUPS):
        ssd_cols += list(range(z_a + g * 512, z_a + (g + 1) * 512))
        ssd_cols += list(range(x_a + g * 512, x_a + (g + 1) * 512))
        ssd_cols += list(range(b_a + g * 128, b_a + (g + 1) * 128))
        ssd_cols += list(range(c_a + g * 128, c_a + (g + 1) * 128))
    for h in range(GDN_HEADS):
        for base in (q_b, k_b, v_b, z_b):
            gdn_cols += list(range(base + h * 128, base + (h + 1) * 128))
    small_dst, small_src = [], []
    for g in range(SSD_GROUPS):
        for dr in range(2):
            for j in range(SSD_HEADS_PER_GROUP):
                small_dst.append(g * SMALL_COLS + dr * SSD_HEADS_PER_GROUP + j)
                small_src.append(dt_a + dr * SSD_HEADS + g * SSD_HEADS_PER_GROUP + j)
    for hbk in range(GDN_HEADS // GDN_HB):
        for kind, base in enumerate((beta_b, a_b)):
            for dr in range(2):
                for j in range(GDN_HB):
                    small_dst.append((SSD_GROUPS + hbk) * SMALL_COLS + (2 * kind + dr) * GDN_HB + j)
                    small_src.append(base + dr * GDN_HEADS + hbk * GDN_HB + j)
    ssd_conv = []
    for g in range(SSD_GROUPS):
        ssd_conv += list(range(g * 512, (g + 1) * 512))
        ssd_conv += list(range(2048 + g * 128, 2048 + (g + 1) * 128))
        ssd_conv += list(range(2560 + g * 128, 2560 + (g + 1) * 128))
    gdn_conv = []
    for h in range(GDN_HEADS):
        for base in (0, 2048, 4096):
            gdn_conv += list(range(base + h * 128, base + (h + 1) * 128))
    return (np.array(ssd_cols), np.array(gdn_cols), np.array(small_dst), np.array(small_src),
            np.array(ssd_conv), np.array(gdn_conv))


def _even_layer_params(j, w_in, ssd_conv_w, ssd_conv_b, ssd_a_log, ssd_dt_bias, ssd_d, ssd_norm,
                       gdn_conv_w, gdn_a_log, gdn_dt_bias, gdn_norm, w_out):
    ssd_cols, gdn_cols, small_dst, small_src, ssd_conv, gdn_conv = _in_proj_layout()
    w = w_in[j]
    w_small = jnp.zeros((D_MODEL, SMALLS_COLS), F32).at[:, small_dst].set(w[:, small_src])

    def ssd_lanes(v):
        v = v.reshape(2, SSD_GROUPS, SSD_HEADS_PER_GROUP).transpose(1, 0, 2).reshape(SSD_GROUPS, 16)
        return jnp.pad(v, ((0, 0), (0, SMALL_COLS - 16))).reshape(1, SSD_GROUPS * SMALL_COLS)

    def gdn_lanes(v):
        nb = GDN_HEADS // GDN_HB
        v = v.reshape(2, nb, GDN_HB).transpose(1, 0, 2).reshape(nb, 2 * GDN_HB)
        return jnp.pad(v, ((0, 0), (2 * GDN_HB, SMALL_COLS - 4 * GDN_HB))).reshape(1, nb * SMALL_COLS)

    return {
        "w_ssd": w[:, ssd_cols].astype(BF16),
        "w_gdn": w[:, gdn_cols].astype(BF16),
        "w_small": w_small.astype(BF16),
        "ssd_cw": ssd_conv_w[j][:, ssd_conv],
        "ssd_cb": ssd_conv_b[j][ssd_conv].reshape(1, -1),
        "ssd_alog": ssd_lanes(ssd_a_log[j]),
        "ssd_dtb": ssd_lanes(ssd_dt_bias[j]),
        "ssd_dskip": jnp.repeat(ssd_d[j], SSD_HEAD_DIM).reshape(1, SSD_D_INNER),
        "ssd_norm": ssd_norm[j].reshape(1, SSD_D_INNER),
        "gdn_cw": gdn_conv_w[j][:, gdn_conv],
        "gdn_alog": gdn_lanes(gdn_a_log[j]),
        "gdn_dtb": gdn_lanes(gdn_dt_bias[j]),
        "gdn_norm": gdn_norm[j].reshape(1, GDN_HEAD_V),
        "w_out_a": w_out[j][:SSD_D_INNER].astype(BF16),
        "w_out_b": w_out[j][SSD_D_INNER:].astype(BF16),
    }


def _mlp(x2, gain, w1_bf, w2_bf):
    h = _rmsnorm(x2, gain, BF16)
    hid = _matmul([h], [w1_bf], BF16, tm=1024, tn=1024, relu2=True, name="ffn_up")
    return _matmul([hid], [w2_bf], F32, tm=512, tn=512, res=x2, name="ffn_down")


def _trunk(x, mix_norm, even_params, pool_w_bf, pool_scale, ffn_norm, w_ff1_bf, w_ff2_bf, final_norm):
    b, l, d = x.shape
    t = b * l
    x2 = x.reshape(t, d)
    for i in range(DEPTH):
        j = i // 2
        if i % 2 == 0:
            prm = even_params[j]
            h = _rmsnorm(x2, mix_norm[i], BF16)
            pssd = _matmul([h], [prm["w_ssd"]], PROJ_DTYPE, tm=1024, tn=1024, slab_out=True, name="in_proj_ssd")
            pgdn = _matmul([h], [prm["w_gdn"]], PROJ_DTYPE, tm=1024, tn=1024, slab_out=True, name="in_proj_gdn")
            psm = _matmul([h], [prm["w_small"]], F32, tm=1024, tn=SMALLS_COLS, name="in_proj_small")
            psm = psm.reshape(b, l, SMALLS_COLS)
            pssd = pssd.reshape(SSD_PROJ_COLS // LANES, b, l, LANES)
            pgdn = pgdn.reshape(GDN_PROJ_COLS // LANES, b, l, LANES)
            ya = _ssd_mixer(pssd, psm, prm).reshape(t, SSD_D_INNER)
            yb = _gdn_mixer(pgdn, psm, prm).reshape(t, GDN_HEADS * GDN_HEAD_V)
            x2 = _matmul([ya, yb], [prm["w_out_a"], prm["w_out_b"]], F32, tm=512, tn=1024, res=x2,
                         name="out_proj")
        else:
            x2 = _pool_layer(x2.reshape(b, l, d), mix_norm[i], pool_w_bf[j], pool_scale[j]).reshape(t, d)
        x2 = _mlp(x2, ffn_norm[i], w_ff1_bf[i], w_ff2_bf[i])
    return _rmsnorm(x2, final_norm, F32).reshape(b, l, d)


def kernel(x_prompt, x_sample, mix_norm, w_in, ssd_conv_w, ssd_conv_b, ssd_a_log, ssd_dt_bias, ssd_d, ssd_norm, gdn_conv_w, gdn_a_log, gdn_dt_bias, gdn_norm, w_out, pool_w, pool_scale, ffn_norm, w_ff1, w_ff2, final_norm):
    even_params = [
        _even_layer_params(j, w_in, ssd_conv_w, ssd_conv_b, ssd_a_log, ssd_dt_bias, ssd_d, ssd_norm,
                           gdn_conv_w, gdn_a_log, gdn_dt_bias, gdn_norm, w_out)
        for j in range(w_in.shape[0])
    ]
    pool_w_bf = pool_w.astype(BF16)
    w_ff1_bf = w_ff1.astype(BF16)
    w_ff2_bf = w_ff2.astype(BF16)
    args = (mix_norm, even_params, pool_w_bf, pool_scale, ffn_norm, w_ff1_bf, w_ff2_bf, final_norm)
    return (_trunk(x_prompt, *args), _trunk(x_sample, *args))
```

```python
import functools

import numpy as np
import jax
import jax.numpy as jnp
from jax import lax
from jax.experimental import pallas as pl
from jax.experimental.pallas import tpu as pltpu

F32 = jnp.float32
BF16 = jnp.bfloat16

D_MODEL = 2048
DEPTH = 2
SSD_D_INNER = 2048
SSD_HEAD_DIM = 64
SSD_HEADS = 32
SSD_GROUPS = 4
SSD_HEADS_PER_GROUP = 8
SSD_STATE = 128
SSD_CHUNK = 128
GDN_HEADS = 16
GDN_HEAD_K = 128
GDN_HEAD_V = 128
CONV_W = 5
POOL_WINDOWS = (2, 4, 8, 16)
POOL_GROUP_DIM = 512
D_FF = 8192
EPS = 1e-6

LANES = 128
HALO = 16
SSD_GROUP_COLS = 1280
SSD_CONV_COLS = 768
GDN_HB = 16
SSD_GB = 4
GDN_SKEW = 1
GDN_HEAD_COLS = 512
GDN_CONV_COLS = 384
GDN_CHUNK = 64
GDN_TILE = 2 * GDN_CHUNK
GDN_SQUARINGS = 5
assert 2 ** (GDN_SQUARINGS + 1) == GDN_CHUNK
SMALL_COLS = 128
SSD_PROJ_COLS = SSD_GROUPS * SSD_GROUP_COLS
GDN_PROJ_COLS = GDN_HEADS * GDN_HEAD_COLS
SMALLS_COLS = SSD_GROUPS * SMALL_COLS + (GDN_HEADS // GDN_HB) * SMALL_COLS
PROJ_DTYPE = BF16
SSD_GROUP_SLABS = SSD_GROUP_COLS // LANES
SSD_CONV_SLABS = SSD_CONV_COLS // LANES
GDN_HEAD_SLABS = GDN_HEAD_COLS // LANES

V7X_VMEM_BYTES = 64 * 1024 * 1024
VMEM_COMPILER_SCRATCH_BYTES = 12 * 1024 * 1024


def _vmem_limit(pipelined_bytes, scratch_bytes=0):
    need = 2 * pipelined_bytes + scratch_bytes + VMEM_COMPILER_SCRATCH_BYTES
    return int(min(need, V7X_VMEM_BYTES - 4 * 1024 * 1024))


def _nbytes(shape, dtype):
    return int(np.prod(shape)) * jnp.dtype(dtype).itemsize


def _dot(a, b):
    return jnp.dot(a, b, preferred_element_type=F32)


def _dot_nt(a, b):
    return lax.dot_general(a, b, (((1,), (1,)), ((), ())), preferred_element_type=F32)


def _split3(x):
    hi = x.astype(BF16)
    r = x - hi.astype(F32)
    mid = r.astype(BF16)
    lo = (r - mid.astype(F32)).astype(BF16)
    return hi, mid, lo


def _dot_sel_lhs(sel, x):
    hi, mid, lo = _split3(x)
    return (_dot(sel, hi) + _dot(sel, mid)) + _dot(sel, lo)


def _dot_sel_rhs(x, sel):
    hi, mid, lo = _split3(x)
    return (_dot(hi, sel) + _dot(mid, sel)) + _dot(lo, sel)


def _sigmoid(x):
    return 1.0 / (1.0 + jnp.exp(-x))


def _silu(x):
    return x * _sigmoid(x)


def _softplus(x):
    return jnp.maximum(x, 0.0) + jnp.log1p(jnp.exp(-jnp.abs(x)))


def _rms(x, w):
    return x * lax.rsqrt(jnp.mean(x * x, axis=-1, keepdims=True) + EPS) * w


def _rmsnorm_kernel(x_ref, g_ref, o_ref):
    o_ref[...] = _rms(x_ref[...], g_ref[...]).astype(o_ref.dtype)


def _rmsnorm(x, gain, out_dtype, tm=512):
    t, d = x.shape
    blocks = _nbytes((tm, d), F32) + _nbytes((tm, d), out_dtype)
    return pl.pallas_call(
        _rmsnorm_kernel,
        grid=(t // tm,),
        in_specs=[pl.BlockSpec((tm, d), lambda i: (i, 0)),
                  pl.BlockSpec((1, d), lambda i: (0, 0))],
        out_specs=pl.BlockSpec((tm, d), lambda i: (i, 0)),
        out_shape=jax.ShapeDtypeStruct((t, d), out_dtype),
        compiler_params=pltpu.CompilerParams(
            dimension_semantics=("arbitrary",), vmem_limit_bytes=_vmem_limit(blocks)),
        name="rmsnorm",
    )(x, gain.reshape(1, d))


def _matmul_kernel(*refs, n_pairs, relu2, has_res, slab_out):
    a_refs = refs[:n_pairs]
    w_refs = refs[n_pairs:2 * n_pairs]
    res_ref = refs[2 * n_pairs] if has_res else None
    o_ref = refs[-1]
    acc = _dot(a_refs[0][...], w_refs[0][...])
    for p in range(1, n_pairs):
        acc = acc + _dot(a_refs[p][...], w_refs[p][...])
    if relu2:
        acc = jnp.square(jnp.maximum(acc, 0.0))
    if has_res:
        acc = acc + res_ref[...]
    if slab_out:
        for s in range(o_ref.shape[0]):
            o_ref[s] = acc[:, s * LANES:(s + 1) * LANES].astype(o_ref.dtype)
    else:
        o_ref[...] = acc.astype(o_ref.dtype)


def _matmul(a_list, w_list, out_dtype, tm, tn, relu2=False, res=None, slab_out=False, name="matmul"):
    t = a_list[0].shape[0]
    n = w_list[0].shape[1]
    in_specs, blocks = [], 0
    for a in a_list:
        in_specs.append(pl.BlockSpec((tm, a.shape[1]), lambda i, j: (i, 0)))
        blocks += _nbytes((tm, a.shape[1]), a.dtype)
    for w in w_list:
        in_specs.append(pl.BlockSpec((w.shape[0], tn), lambda i, j: (0, j)))
        blocks += _nbytes((w.shape[0], tn), w.dtype)
    args = list(a_list) + list(w_list)
    if res is not None:
        in_specs.append(pl.BlockSpec((tm, tn), lambda i, j: (i, j)))
        blocks += _nbytes((tm, tn), res.dtype)
        args.append(res)
    blocks += _nbytes((tm, tn), out_dtype)
    kern = functools.partial(_matmul_kernel, n_pairs=len(a_list), relu2=relu2, has_res=res is not None,
                             slab_out=slab_out)
    if slab_out:
        out_spec = pl.BlockSpec((tn // LANES, tm, LANES), lambda i, j: (j, i, 0))
        out_shape = jax.ShapeDtypeStruct((n // LANES, t, LANES), out_dtype)
    else:
        out_spec = pl.BlockSpec((tm, tn), lambda i, j: (i, j))
        out_shape = jax.ShapeDtypeStruct((t, n), out_dtype)
    return pl.pallas_call(
        kern,
        grid=(t // tm, n // tn),
        in_specs=in_specs,
        out_specs=out_spec,
        out_shape=out_shape,
        compiler_params=pltpu.CompilerParams(
            dimension_semantics=("arbitrary", "arbitrary"),
            vmem_limit_bytes=_vmem_limit(blocks, _nbytes((tm, tn), F32))),
        name=name,
    )(*args)


def _ssd_kernel(*refs, rev, nc):
    if rev:
        (main, prv, nxt, sm, cw, cb, alog, dtb, dsk, nw, yprev, out, xw, st) = refs
    else:
        (main, prv, nxt, sm, cw, cb, alog, dtb, out, xw, st) = refs
    q = SSD_CHUNK
    c = pl.program_id(2)
    cc = (nc - 1 - c) if rev else c
    d = 1 if rev else 0

    @pl.when(c == 0)
    def _():
        st[...] = jnp.zeros_like(st)

    z0 = SSD_GROUP_SLABS - SSD_CONV_SLABS
    rowi = lax.broadcasted_iota(jnp.int32, (q, q), 0)
    lane = lax.broadcasted_iota(jnp.int32, (q, q), 1)
    tri = (lane >= rowi) if rev else (lane <= rowi)
    tri_bf = jnp.where(tri, 1.0, 0.0).astype(BF16)
    er = lax.broadcasted_iota(jnp.int32, (SMALL_COLS, 512), 0)
    eq = lax.broadcasted_iota(jnp.int32, (SMALL_COLS, 512), 1)
    expand = jnp.where(er == d * 8 + lax.shift_right_logical(eq, 6), 1.0, 0.0).astype(BF16)
    zero_h = jnp.zeros((SSD_CONV_SLABS, HALO, LANES), F32)

    def group_stages(gi):
        m0 = gi * SSD_GROUP_SLABS
        w0 = gi * SSD_CONV_SLABS
        xw[w0:w0 + SSD_CONV_SLABS, 0:HALO, :] = jnp.where(
            cc == 0, zero_h, prv[m0 + z0:m0 + SSD_GROUP_SLABS].astype(F32))
        xw[w0:w0 + SSD_CONV_SLABS, HALO:HALO + q, :] = main[m0 + z0:m0 + SSD_GROUP_SLABS].astype(F32)
        xw[w0:w0 + SSD_CONV_SLABS, HALO + q:HALO + q + HALO, :] = jnp.where(
            cc == nc - 1, zero_h, nxt[m0 + z0:m0 + SSD_GROUP_SLABS].astype(F32))
        conv = []
        for s in range(SSD_CONV_SLABS):
            c0 = gi * SSD_CONV_COLS + s * LANES
            acc = cb[:, c0:c0 + LANES]
            for k in range(CONV_W):
                acc = acc + cw[k:k + 1, c0:c0 + LANES] * xw[w0 + s, pl.ds(HALO - CONV_W // 2 + k, q), :]
            conv.append(_silu(acc))
        xs = jnp.concatenate(conv[0:4], axis=1)
        bm = conv[4]
        cm = conv[5]

        l0 = gi * SMALL_COLS
        dtv = _softplus(sm[:, l0:l0 + SMALL_COLS] + dtb[:, l0:l0 + SMALL_COLS])
        ac = dtv * (-jnp.exp(alog[:, l0:l0 + SMALL_COLS]))
        cs = _dot_sel_lhs(tri_bf, ac)
        tot = cs[0:1, :] if rev else cs[q - 1:q, :]
        src_t = (cs - jnp.log(dtv)).T
        to_end = dtv * jnp.exp(tot - cs)
        yield
        to_end_e = _dot(to_end.astype(BF16), expand)
        chunk_decay = _dot_sel_rhs(jnp.broadcast_to(jnp.exp(tot), (8, SMALL_COLS)), expand)[0:1, :]
        g = _dot_nt(cm.astype(BF16), bm.astype(BF16))
        s_old = st[gi]
        s_bf = s_old.astype(BF16)
        xs_bf = xs.astype(BF16)
        yield
        new_state = _dot(bm.T.astype(BF16), (xs * to_end_e).astype(BF16))
        st[gi] = s_old * chunk_decay + new_state
        y0 = gi * 512
        for p in range(SSD_HEADS_PER_GROUP // 2):
            rhs = jnp.concatenate([xs_bf[:, p * LANES:(p + 1) * LANES], s_bf[:, p * LANES:(p + 1) * LANES]],
                                  axis=0)
            ys = []
            for t in range(2):
                ch = d * 8 + 2 * p + t
                colb = jnp.broadcast_to(cs[:, ch:ch + 1], (q, q))
                lmat = jnp.exp(jnp.where(tri, colb - src_t[ch:ch + 1, :], -jnp.inf))
                lhs = jnp.concatenate([(g * lmat).astype(BF16), (cm * jnp.exp(colb)).astype(BF16)], axis=1)
                ys.append(_dot(lhs, rhs))
            y = jnp.where(lane < SSD_HEAD_DIM, ys[0], ys[1])
            if rev:
                o0 = y0 + p * LANES
                ytot = yprev[:, o0:o0 + LANES] + y + dsk[:, o0:o0 + LANES] * conv[p]
                gated[gi][p] = ytot * _silu(main[m0 + p].astype(F32))
            else:
                out[:, y0 + p * LANES:y0 + (p + 1) * LANES] = y
            if p % 2 == 1:
                yield
        if rev:
            yt = jnp.concatenate(gated[gi], axis=1)
            out[:, y0:y0 + 512] = _rms(yt, nw[:, y0:y0 + 512]).astype(out.dtype)

    gated = [[None] * (SSD_HEADS_PER_GROUP // 2) for _ in range(SSD_GB)]
    active = [group_stages(gi) for gi in range(SSD_GB)]
    while active:
        for gen in list(active):
            if next(gen, StopIteration) is StopIteration:
                active.remove(gen)


def _ssd_mixer(pbig, psm, prm):
    _, b, l, _ = pbig.shape
    nc = l // SSD_CHUNK
    hb = SSD_CHUNK // HALO
    nhb = l // HALO

    def cidx(c, rev):
        return (nc - 1 - c) if rev else c

    gs = SSD_GB * SSD_GROUP_SLABS
    ycols = SSD_GB * 512

    def specs(rev):
        return [
            pl.BlockSpec((gs, None, SSD_CHUNK, LANES), lambda bi, g, c: (g, bi, cidx(c, rev), 0)),
            pl.BlockSpec((gs, None, HALO, LANES),
                         lambda bi, g, c: (g, bi, jnp.maximum(cidx(c, rev) * hb - 1, 0), 0)),
            pl.BlockSpec((gs, None, HALO, LANES),
                         lambda bi, g, c: (g, bi, jnp.minimum((cidx(c, rev) + 1) * hb, nhb - 1), 0)),
            pl.BlockSpec((None, SSD_CHUNK, SSD_GB * SMALL_COLS), lambda bi, g, c: (bi, cidx(c, rev), g)),
            pl.BlockSpec((CONV_W, SSD_GB * SSD_CONV_COLS), lambda bi, g, c: (0, g)),
            pl.BlockSpec((1, SSD_GB * SSD_CONV_COLS), lambda bi, g, c: (0, g)),
            pl.BlockSpec((1, SSD_GB * SMALL_COLS), lambda bi, g, c: (0, g)),
            pl.BlockSpec((1, SSD_GB * SMALL_COLS), lambda bi, g, c: (0, g)),
        ]

    def yspec(rev):
        return pl.BlockSpec((None, SSD_CHUNK, ycols), lambda bi, g, c: (bi, cidx(c, rev), g))

    window = (SSD_GB * SSD_CONV_SLABS, SSD_CHUNK + 2 * HALO, LANES)
    scratch = [pltpu.VMEM(window, F32),
               pltpu.VMEM((SSD_GB, SSD_STATE, 512), F32)]
    blocks = (SSD_GB * _nbytes((SSD_CHUNK + 2 * HALO, SSD_GROUP_COLS), pbig.dtype)
              + SSD_GB * _nbytes((SSD_CHUNK, SMALL_COLS), F32) + 2 * _nbytes((SSD_CHUNK, ycols), F32))
    cparams = pltpu.CompilerParams(
        dimension_semantics=("arbitrary", "arbitrary", "arbitrary"),
        vmem_limit_bytes=_vmem_limit(blocks, _nbytes(window, F32) + _nbytes((SSD_GB, SSD_STATE, 512), F32)))
    common = (pbig, pbig, pbig, psm, prm["ssd_cw"], prm["ssd_cb"], prm["ssd_alog"], prm["ssd_dtb"])
    y_fwd = pl.pallas_call(
        functools.partial(_ssd_kernel, rev=False, nc=nc),
        grid=(b, SSD_GROUPS // SSD_GB, nc),
        in_specs=specs(False),
        out_specs=yspec(False),
        out_shape=jax.ShapeDtypeStruct((b, l, SSD_D_INNER), F32),
        scratch_shapes=scratch,
        compiler_params=cparams,
        name="ssd_fwd",
    )(*common)
    return pl.pallas_call(
        functools.partial(_ssd_kernel, rev=True, nc=nc),
        grid=(b, SSD_GROUPS // SSD_GB, nc),
        in_specs=specs(True) + [pl.BlockSpec((1, ycols), lambda bi, g, c: (0, g)),
                                pl.BlockSpec((1, ycols), lambda bi, g, c: (0, g)),
                                yspec(True)],
        out_specs=yspec(True),
        out_shape=jax.ShapeDtypeStruct((b, l, SSD_D_INNER), BF16),
        scratch_shapes=scratch,
        compiler_params=cparams,
        name="ssd_bwd",
    )(*common, prm["ssd_dskip"], prm["ssd_norm"], y_fwd)


def _gdn_kernel(*refs, rev, nc):
    if rev:
        (main, prv, nxt, sm, cw, alog, dtb, nw, oprev, out, xw, st,
         p_kbq, p_k, p_vbk, p_qdec, p_kdt, p_dec, p_egl, p_z) = refs
        prepared = (p_kbq, p_k, p_vbk, p_qdec, p_kdt, p_dec, p_egl, p_z)
    else:
        (main, prv, nxt, sm, cw, alog, dtb, out, xw, st,
         p_kbq, p_k, p_vbk, p_qdec, p_kdt, p_dec, p_egl) = refs
        prepared = (p_kbq, p_k, p_vbk, p_qdec, p_kdt, p_dec, p_egl)
    n = GDN_TILE
    q = GDN_CHUNK
    c = pl.program_id(2)
    tp = jnp.minimum(c, nc - 1)
    cc = (nc - 1 - tp) if rev else tp
    d = 1 if rev else 0

    @pl.when(c == 0)
    def _():
        st[...] = jnp.zeros_like(st)
        for ref in prepared:
            ref[...] = jnp.zeros_like(ref)

    rowi = lax.broadcasted_iota(jnp.int32, (n, n), 0)
    lane = lax.broadcasted_iota(jnp.int32, (n, n), 1)
    same = lax.shift_right_logical(rowi, 6) == lax.shift_right_logical(lane, 6)
    tri = same & ((lane >= rowi) if rev else (lane <= rowi))
    strict = same & ((lane > rowi) if rev else (lane < rowi))
    eye = jnp.where(rowi == lane, 1.0, 0.0)
    zeros_q = jnp.zeros((q, GDN_HEAD_V), F32)
    egl_prev = p_egl[...]

    def chain_stages(j):
        lg = 2 * GDN_HB + d * GDN_HB + j
        dec = p_dec[j]
        kq = _dot_nt(p_kbq[j], p_k[j])
        vbk = p_vbk[j]
        qdec_bf = p_qdec[j]
        kdec_t = p_kdt[j]
        zh = p_z[j] if rev else None
        yield "loaded"
        m = jnp.where(strict, kq[0:n] * dec, 0.0)
        attn_bf = jnp.where(tri, kq[n:2 * n] * dec, 0.0).astype(BF16)
        tinv = eye - m
        mk_bf = m.astype(BF16)
        yield
        mk_bf = _dot(mk_bf, mk_bf).astype(BF16)
        for _ in range(GDN_SQUARINGS - 1):
            yield
            mk_next = _dot(mk_bf, mk_bf)
            tinv = tinv + _dot(tinv.astype(BF16), mk_bf)
            mk_bf = mk_next.astype(BF16)
        yield
        tinv = tinv + _dot(tinv.astype(BF16), mk_bf)
        yield
        uw = _dot(tinv.astype(BF16), vbk)
        u = uw[:, 0:GDN_HEAD_V]
        w_bf = uw[:, GDN_HEAD_V:2 * GDN_HEAD_V].astype(BF16)
        s = st[j]
        vns = [None, None]
        oqs = [None, None]
        for ci in ((1, 0) if rev else (0, 1)):
            r0 = ci * q
            yield
            res = _dot(jnp.concatenate([w_bf[r0:r0 + q], qdec_bf[r0:r0 + q]], axis=0), s.astype(BF16))
            vns[ci] = u[r0:r0 + q] - res[0:q]
            oqs[ci] = res[q:n]
            vnpad = jnp.concatenate([vns[ci], zeros_q] if ci == 0 else [zeros_q, vns[ci]], axis=0)
            yield
            s = s * egl_prev[8 * ci:8 * ci + 1, lg:lg + 1] + _dot(kdec_t, vnpad.astype(BF16))
        st[j] = s
        yield
        o = jnp.concatenate(oqs, axis=0) + _dot(attn_bf, jnp.concatenate(vns, axis=0).astype(BF16))
        if rev:
            ot = _rms(oprev[:, j * GDN_HEAD_V:(j + 1) * GDN_HEAD_V] + o, nw[...])
            out[:, j * GDN_HEAD_V:(j + 1) * GDN_HEAD_V] = (ot * _silu(zh.astype(F32))).astype(out.dtype)
        else:
            out[:, j * GDN_HEAD_V:(j + 1) * GDN_HEAD_V] = o

    zero_h = jnp.zeros((GDN_HB * GDN_HEAD_SLABS, HALO, LANES), F32)
    xw[:, 0:HALO, :] = jnp.where(cc == 0, zero_h, prv[...].astype(F32))
    xw[:, HALO:HALO + n, :] = main[...].astype(F32)
    xw[:, HALO + n:HALO + n + HALO, :] = jnp.where(cc == nc - 1, zero_h, nxt[...].astype(F32))
    smv = sm[...]
    beta_all = _sigmoid(smv)
    g_all = -jnp.exp(alog[...]) * _softplus(smv + dtb[...])
    gc = _dot_sel_lhs(jnp.where(tri, 1.0, 0.0).astype(BF16), g_all)
    gc_t = gc.T
    first_last, second_last = (0, q) if rev else (q - 1, n - 1)
    gl = jnp.where(rowi < q, gc[first_last:first_last + 1, :], gc[second_last:second_last + 1, :])
    egc = jnp.exp(gc)
    ekd = jnp.exp(gl - gc)
    egl = jnp.exp(gl)

    def prepare(j):
        s0 = j * GDN_HEAD_SLABS
        conv = []
        for s in range(GDN_CONV_COLS // LANES):
            c0 = j * GDN_CONV_COLS + s * LANES
            acc = cw[0:1, c0:c0 + LANES] * xw[s0 + s, pl.ds(HALO - CONV_W // 2, n), :]
            for k in range(1, CONV_W):
                acc = acc + cw[k:k + 1, c0:c0 + LANES] * xw[s0 + s, pl.ds(HALO - CONV_W // 2 + k, n), :]
            conv.append(_silu(acc))
        qh, kh, vh = conv
        qh = qh * (lax.rsqrt(jnp.sum(qh * qh, axis=-1, keepdims=True) + EPS) * (GDN_HEAD_K ** -0.5))
        kh = kh * lax.rsqrt(jnp.sum(kh * kh, axis=-1, keepdims=True) + EPS)
        lb = d * GDN_HB + j
        lg = 2 * GDN_HB + d * GDN_HB + j
        beta = beta_all[:, lb:lb + 1]
        egc_col = egc[:, lg:lg + 1]
        kb = kh * beta
        p_dec[j] = jnp.exp(jnp.where(tri, gc[:, lg:lg + 1] - gc_t[lg:lg + 1, :], -jnp.inf))
        p_kbq[j] = jnp.concatenate([kb, qh], axis=0).astype(BF16)
        p_k[j] = kh.astype(BF16)
        p_vbk[j] = jnp.concatenate([(vh * beta).astype(BF16), (kb * egc_col).astype(BF16)], axis=1)
        p_qdec[j] = (qh * egc_col).astype(BF16)
        p_kdt[j] = (kh * ekd[:, lg:lg + 1]).T.astype(BF16)
        if rev:
            p_z[j] = main[s0 + GDN_HEAD_SLABS - 1]

    active = [chain_stages(j) for j in range(GDN_HB)]
    heads_loaded = 0
    while active:
        for gen in list(active):
            state = next(gen, StopIteration)
            if state is StopIteration:
                active.remove(gen)
            elif state == "loaded":
                prepare(heads_loaded)
                heads_loaded += 1
    p_egl[...] = jnp.concatenate([egl[0:8, :], egl[q:q + 8, :]], axis=0)


def _gdn_mixer(pbig, psm, prm):
    _, b, l, _ = pbig.shape
    nc = l // GDN_TILE
    hb = GDN_TILE // HALO
    nhb = l // HALO
    nblk = GDN_HEADS // GDN_HB
    slabs = GDN_HB * GDN_HEAD_SLABS
    sm0 = SSD_GROUPS

    def prep_tile(c, rev):
        t = jnp.minimum(c, nc - 1)
        return (nc - 1 - t) if rev else t

    def chain_tile(c, rev):
        t = jnp.maximum(c - 1, 0)
        return (nc - 1 - t) if rev else t

    def specs(rev):
        return [
            pl.BlockSpec((slabs, None, GDN_TILE, LANES), lambda bi, h, c: (h, bi, prep_tile(c, rev), 0)),
            pl.BlockSpec((slabs, None, HALO, LANES),
                         lambda bi, h, c: (h, bi, jnp.maximum(prep_tile(c, rev) * hb - 1, 0), 0)),
            pl.BlockSpec((slabs, None, HALO, LANES),
                         lambda bi, h, c: (h, bi, jnp.minimum((prep_tile(c, rev) + 1) * hb, nhb - 1), 0)),
            pl.BlockSpec((None, GDN_TILE, SMALL_COLS), lambda bi, h, c: (bi, prep_tile(c, rev), sm0 + h)),
            pl.BlockSpec((CONV_W, GDN_HB * GDN_CONV_COLS), lambda bi, h, c: (0, h)),
            pl.BlockSpec((1, SMALL_COLS), lambda bi, h, c: (0, h)),
            pl.BlockSpec((1, SMALL_COLS), lambda bi, h, c: (0, h)),
        ]

    def ospec(rev):
        return pl.BlockSpec((None, GDN_TILE, GDN_HB * GDN_HEAD_V), lambda bi, h, c: (bi, chain_tile(c, rev), h))

    window = (slabs, GDN_TILE + 2 * HALO, LANES)
    tile = (GDN_HB, GDN_TILE, LANES)

    def scratch(rev):
        shapes = [pltpu.VMEM(window, F32),
                  pltpu.VMEM((GDN_HB, GDN_HEAD_K, GDN_HEAD_V), F32),
                  pltpu.VMEM((GDN_HB, 2 * GDN_TILE, LANES), BF16),
                  pltpu.VMEM(tile, BF16),
                  pltpu.VMEM((GDN_HB, GDN_TILE, 2 * LANES), BF16),
                  pltpu.VMEM(tile, BF16),
                  pltpu.VMEM(tile, BF16),
                  pltpu.VMEM(tile, F32),
                  pltpu.VMEM((16, SMALL_COLS), F32)]
        if rev:
            shapes.append(pltpu.VMEM(tile, pbig.dtype))
        return shapes

    scratch_bytes = (_nbytes(window, F32) + _nbytes((GDN_HB, GDN_HEAD_K, GDN_HEAD_V), F32)
                     + 8 * _nbytes(tile, BF16) + _nbytes(tile, F32))
    blocks = (_nbytes(window, pbig.dtype)
              + _nbytes((GDN_TILE, SMALL_COLS), F32) + 2 * _nbytes((GDN_TILE, GDN_HB * GDN_HEAD_V), F32))
    cparams = pltpu.CompilerParams(
        dimension_semantics=("arbitrary", "arbitrary", "arbitrary"),
        vmem_limit_bytes=_vmem_limit(blocks, scratch_bytes))
    common = (pbig, pbig, pbig, psm, prm["gdn_cw"], prm["gdn_alog"], prm["gdn_dtb"])
    o_fwd = pl.pallas_call(
        functools.partial(_gdn_kernel, rev=False, nc=nc),
        grid=(b, nblk, nc + 1),
        in_specs=specs(False),
        out_specs=ospec(False),
        out_shape=jax.ShapeDtypeStruct((b, l, GDN_HEADS * GDN_HEAD_V), F32),
        scratch_shapes=scratch(False),
        compiler_params=cparams,
        name="gdn_fwd",
    )(*common)
    return pl.pallas_call(
        functools.partial(_gdn_kernel, rev=True, nc=nc),
        grid=(b, nblk, nc + 1),
        in_specs=specs(True) + [pl.BlockSpec((1, GDN_HEAD_V), lambda bi, h, c: (0, 0)), ospec(True)],
        out_specs=ospec(True),
        out_shape=jax.ShapeDtypeStruct((b, l, GDN_HEADS * GDN_HEAD_V), BF16),
        scratch_shapes=scratch(True),
        compiler_params=cparams,
        name="gdn_bwd",
    )(*common, prm["gdn_norm"], o_fwd)


def _pool_kernel(x, xp, xn, gain, pw, ps, gain_next, out, h_next, hw, *, nt, tm, seq):
    i = pl.program_id(1)
    zero_h = jnp.zeros((HALO, D_MODEL), F32)
    hw[0:HALO, :] = jnp.where(i == 0, zero_h, _rms(xp[...], gain[...]))
    hw[HALO:HALO + tm, :] = _rms(x[...], gain[...])
    hw[HALO + tm:HALO + tm + HALO, :] = jnp.where(i == nt - 1, zero_h, _rms(xn[...], gain[...]))
    t = i * tm + lax.broadcasted_iota(jnp.int32, (tm, 1), 0)
    for gi, win in enumerate(POOL_WINDOWS):
        lo_c = gi * POOL_GROUP_DIM
        hi_c = lo_c + POOL_GROUP_DIM
        half = win // 2
        s = hw[pl.ds(HALO - half, tm), lo_c:hi_c]
        for o in range(-half + 1, half):
            s = s + hw[pl.ds(HALO + o, tm), lo_c:hi_c]
        cnt = (jnp.minimum(t + half, seq) - jnp.maximum(t - half, 0)).astype(F32)
        p = s / cnt - hw[HALO:HALO + tm, lo_c:hi_c]
        y = _dot(p.astype(BF16), pw[gi]) * ps[:, lo_c:hi_c]
        out[:, lo_c:hi_c] = x[:, lo_c:hi_c] + y
    h_next[...] = _rms(out[...], gain_next[...]).astype(h_next.dtype)


def _pool_layer(x3, gain, pool_w_bf, pool_scale, gain_next, tm=256):
    b, l, d = x3.shape
    nt = l // tm
    hb = tm // HALO
    nhb = l // HALO
    blocks = (2 * _nbytes((tm + 2 * HALO, d), F32) + _nbytes((tm, d), BF16)
              + _nbytes(pool_w_bf.shape, BF16))
    return pl.pallas_call(
        functools.partial(_pool_kernel, nt=nt, tm=tm, seq=l),
        grid=(b, nt),
        in_specs=[
            pl.BlockSpec((None, tm, d), lambda bi, i: (bi, i, 0)),
            pl.BlockSpec((None, HALO, d), lambda bi, i: (bi, jnp.maximum(i * hb - 1, 0), 0)),
            pl.BlockSpec((None, HALO, d), lambda bi, i: (bi, jnp.minimum((i + 1) * hb, nhb - 1), 0)),
            pl.BlockSpec((1, d), lambda bi, i: (0, 0)),
            pl.BlockSpec(pool_w_bf.shape, lambda bi, i: (0, 0, 0)),
            pl.BlockSpec((1, d), lambda bi, i: (0, 0)),
            pl.BlockSpec((1, d), lambda bi, i: (0, 0)),
        ],
        out_specs=[pl.BlockSpec((None, tm, d), lambda bi, i: (bi, i, 0)),
                   pl.BlockSpec((None, tm, d), lambda bi, i: (bi, i, 0))],
        out_shape=[jax.ShapeDtypeStruct((b, l, d), F32), jax.ShapeDtypeStruct((b, l, d), BF16)],
        scratch_shapes=[pltpu.VMEM((tm + 2 * HALO, d), F32)],
        compiler_params=pltpu.CompilerParams(
            dimension_semantics=("arbitrary", "arbitrary"),
            vmem_limit_bytes=_vmem_limit(blocks, _nbytes((tm + 2 * HALO, d), F32))),
        name="pool_mixer",
    )(x3, x3, x3, gain.reshape(1, d), pool_w_bf, pool_scale.reshape(1, d), gain_next.reshape(1, d))


def _in_proj_layout():
    z_a, x_a, b_a, c_a, dt_a = 0, 2048, 4096, 4608, 5120
    q_b, k_b, v_b, z_b, beta_b, a_b = 5184, 7232, 9280, 11328, 13376, 13408
    ssd_cols, gdn_cols = [], []
    for g in range(SSD_GROUPS):
        ssd_cols += list(range(z_a + g * 512, z_a + (g + 1) * 512))
        ssd_cols += list(range(x_a + g * 512, x_a + (g + 1) * 512))
        ssd_cols += list(range(b_a + g * 128, b_a + (g + 1) * 128))
        ssd_cols += list(range(c_a + g * 128, c_a + (g + 1) * 128))
    for h in range(GDN_HEADS):
        for base in (q_b, k_b, v_b, z_b):
            gdn_cols += list(range(base + h * 128, base + (h + 1) * 128))
    small_dst, small_src = [], []
    for g in range(SSD_GROUPS):
        for dr in range(2):
            for j in range(SSD_HEADS_PER_GROUP):
                small_dst.append(g * SMALL_COLS + dr * SSD_HEADS_PER_GROUP + j)
                small_src.append(dt_a + dr * SSD_HEADS + g * SSD_HEADS_PER_GROUP + j)
    for hbk in range(GDN_HEADS // GDN_HB):
        for kind, base in enumerate((beta_b, a_b)):
            for dr in range(2):
                for j in range(GDN_HB):
                    small_dst.append((SSD_GROUPS + hbk) * SMALL_COLS + (2 * kind + dr) * GDN_HB + j)
                    small_src.append(base + dr * GDN_HEADS + hbk * GDN_HB + j)
    ssd_conv = []
    for g in range(SSD_GROUPS):
        ssd_conv += list(range(g * 512, (g + 1) * 512))
        ssd_conv += list(range(2048 + g * 128, 2048 + (g + 1) * 128))
        ssd_conv += list(range(2560 + g * 128, 2560 + (g + 1) * 128))
    gdn_conv = []
    for h in range(GDN_HEADS):
        for base in (0, 2048, 4096):
            gdn_conv += list(range(base + h * 128, base + (h + 1) * 128))
    return (np.array(ssd_cols), np.array(gdn_cols), np.array(small_dst), np.array(small_src),
            np.array(ssd_conv), np.array(gdn_conv))


def _even_layer_params(j, w_in, ssd_conv_w, ssd_conv_b, ssd_a_log, ssd_dt_bias, ssd_d, ssd_norm,
                       gdn_conv_w, gdn_a_log, gdn_dt_bias, gdn_norm, w_out):
    ssd_cols, gdn_cols, small_dst, small_src, ssd_conv, gdn_conv = _in_proj_layout()
    w = w_in[j]
    w_small = jnp.zeros((D_MODEL, SMALLS_COLS), F32).at[:, small_dst].set(w[:, small_src])

    def ssd_lanes(v):
        v = v.reshape(2, SSD_GROUPS, SSD_HEADS_PER_GROUP).transpose(1, 0, 2).reshape(SSD_GROUPS, 16)
        return jnp.pad(v, ((0, 0), (0, SMALL_COLS - 16))).reshape(1, SSD_GROUPS * SMALL_COLS)

    def gdn_lanes(v):
        nb = GDN_HEADS // GDN_HB
        v = v.reshape(2, nb, GDN_HB).transpose(1, 0, 2).reshape(nb, 2 * GDN_HB)
        return jnp.pad(v, ((0, 0), (2 * GDN_HB, SMALL_COLS - 4 * GDN_HB))).reshape(1, nb * SMALL_COLS)

    return {
        "w_ssd": w[:, ssd_cols].astype(BF16),
        "w_gdn": w[:, gdn_cols].astype(BF16),
        "w_small": w_small.astype(BF16),
        "ssd_cw": ssd_conv_w[j][:, ssd_conv],
        "ssd_cb": ssd_conv_b[j][ssd_conv].reshape(1, -1),
        "ssd_alog": ssd_lanes(ssd_a_log[j]),
        "ssd_dtb": ssd_lanes(ssd_dt_bias[j]),
        "ssd_dskip": jnp.repeat(ssd_d[j], SSD_HEAD_DIM).reshape(1, SSD_D_INNER),
        "ssd_norm": ssd_norm[j].reshape(1, SSD_D_INNER),
        "gdn_cw": gdn_conv_w[j][:, gdn_conv],
        "gdn_alog": gdn_lanes(gdn_a_log[j]),
        "gdn_dtb": gdn_lanes(gdn_dt_bias[j]),
        "gdn_norm": gdn_norm[j].reshape(1, GDN_HEAD_V),
        "w_out_a": w_out[j][:SSD_D_INNER].astype(BF16),
        "w_out_b": w_out[j][SSD_D_INNER:].astype(BF16),
    }


def _mlp(x2, h, w1_bf, w2_bf):
    hid = _matmul([h], [w1_bf], BF16, tm=1024, tn=1024, relu2=True, name="ffn_up")
    return _matmul([hid], [w2_bf], F32, tm=512, tn=512, res=x2, name="ffn_down")


def _trunk(x, mix_norm, even_params, pool_w_bf, pool_scale, ffn_norm, w_ff1_bf, w_ff2_bf, final_norm):
    b, l, d = x.shape
    t = b * l
    x2 = x.reshape(t, d)
    for i in range(DEPTH):
        j = i // 2
        if i % 2 == 0:
            prm = even_params[j]
            h = _rmsnorm(x2, mix_norm[i], BF16)
            pssd = _matmul([h], [prm["w_ssd"]], PROJ_DTYPE, tm=1024, tn=1024, slab_out=True, name="in_proj_ssd")
            pgdn = _matmul([h], [prm["w_gdn"]], PROJ_DTYPE, tm=1024, tn=1024, slab_out=True, name="in_proj_gdn")
            psm = _matmul([h], [prm["w_small"]], F32, tm=1024, tn=SMALLS_COLS, name="in_proj_small")
            psm = psm.reshape(b, l, SMALLS_COLS)
            pssd = pssd.reshape(SSD_PROJ_COLS // LANES, b, l, LANES)
            pgdn = pgdn.reshape(GDN_PROJ_COLS // LANES, b, l, LANES)
            ya = _ssd_mixer(pssd, psm, prm).reshape(t, SSD_D_INNER)
            yb = _gdn_mixer(pgdn, psm, prm).reshape(t, GDN_HEADS * GDN_HEAD_V)
            x2 = _matmul([ya, yb], [prm["w_out_a"], prm["w_out_b"]], F32, tm=512, tn=1024, res=x2,
                         name="out_proj")
            h = _rmsnorm(x2, ffn_norm[i], BF16)
        else:
            x3, h3 = _pool_layer(x2.reshape(b, l, d), mix_norm[i], pool_w_bf[j], pool_scale[j], ffn_norm[i])
            x2 = x3.reshape(t, d)
            h = h3.reshape(t, d)
        x2 = _mlp(x2, h, w_ff1_bf[i], w_ff2_bf[i])
    return _rmsnorm(x2, final_norm, F32).reshape(b, l, d)


def kernel(x_prompt, x_sample, mix_norm, w_in, ssd_conv_w, ssd_conv_b, ssd_a_log, ssd_dt_bias, ssd_d, ssd_norm, gdn_conv_w, gdn_a_log, gdn_dt_bias, gdn_norm, w_out, pool_w, pool_scale, ffn_norm, w_ff1, w_ff2, final_norm):
    even_params = [
        _even_layer_params(j, w_in, ssd_conv_w, ssd_conv_b, ssd_a_log, ssd_dt_bias, ssd_d, ssd_norm,
                           gdn_conv_w, gdn_a_log, gdn_dt_bias, gdn_norm, w_out)
        for j in range(w_in.shape[0])
    ]
    pool_w_bf = pool_w.astype(BF16)
    w_ff1_bf = w_ff1.astype(BF16)
    w_ff2_bf = w_ff2.astype(BF16)
    args = (mix_norm, even_params, pool_w_bf, pool_scale, ffn_norm, w_ff1_bf, w_ff2_bf, final_norm)
    return (_trunk(x_prompt, *args), _trunk(x_sample, *args))
```
